```python
import math
import jax, jax.numpy as jnp
from jax import lax
import numpy as np

D_MODEL = 1024
BATCH = 8
SEQ = 8192
DEPTH = 2

CTX_LEN = 256
GRID_W = 64
D_FF = 2816
N_MOD = 9
ROPE_BASE = 10000.0
Q_BLOCK = 128
NEG_INF = -1e30
EPS = 1e-6
SUBLN_EPS = 1e-5

MLA_HEADS = 4
MLA_Q_RANK = 256
MLA_KV_RANK = 128
MLA_NOPE = 64
MLA_ROPE = 32
MLA_V = 64
MLA_SCALE = (MLA_NOPE + MLA_ROPE) ** -0.5
DIFF_HEADS = 4
DIFF_HEAD = 32
DIFF_V = 2 * DIFF_HEAD
DIFF_SCALE = DIFF_HEAD ** -0.5
NA_HEADS = 4
NA_HEAD = 64
NA_ROWS = 8
NA_COLS = 16
NA_SCALE = NA_HEAD ** -0.5
GQA_HEADS = 4
GQA_KV_HEADS = 2
GQA_HEAD = 64
WINDOW = 128
GQA_SCALE = GQA_HEAD ** -0.5

N_BRANCH = 4
BRANCH_W = 256
DIFF_QK_W = DIFF_HEADS * 2 * DIFF_HEAD
DIFF_V_W = DIFF_HEADS * DIFF_V
NA_W = NA_HEADS * NA_HEAD
GQA_Q_W = GQA_HEADS * GQA_HEAD
GQA_KV_W = GQA_KV_HEADS * GQA_HEAD
IN_SIZES = (MLA_Q_RANK, MLA_KV_RANK, MLA_ROPE,
            DIFF_QK_W, DIFF_QK_W, DIFF_V_W,
            NA_W, NA_W, NA_W,
            GQA_Q_W, GQA_KV_W, GQA_KV_W)
IN_COLS = 2464

kernel_name = "hybrid_parallel_mla_diff_natten_swa_dit"


def rmsnorm(x, g, eps=EPS):
    x32 = x.astype(jnp.float32)
    y = x32 * lax.rsqrt(jnp.mean(x32 * x32, axis=-1, keepdims=True) + eps)
    return (y * g.astype(jnp.float32)).astype(x.dtype)


def modulate(h, g, shift, scale):
    return rmsnorm(h, g) * (1 + scale) + shift


def swiglu(u, w_gu, w_down):
    g, up = jnp.split(u @ w_gu, 2, axis=-1)
    return (jax.nn.silu(g) * up) @ w_down


def softmax_f32(s):
    return jax.nn.softmax(s.astype(jnp.float32), axis=-1)


def axial_angles(n_tok, dim):
    t = jnp.arange(n_tok)
    rows = (t // GRID_W).astype(jnp.float32)
    cols = (t % GRID_W).astype(jnp.float32)
    half = dim // 2
    freqs = jnp.power(ROPE_BASE, -jnp.arange(0, half, 2, dtype=jnp.float32) / half)
    return rows[:, None] * freqs, cols[:, None] * freqs


def rope_1d(x, ang):
    x1, x2 = jnp.split(x, 2, axis=-1)
    cos, sin = jnp.cos(ang), jnp.sin(ang)
    return jnp.concatenate([x1 * cos - x2 * sin, x2 * cos + x1 * sin], axis=-1)


def rope_2d(x, angles):
    ang_r, ang_c = angles
    shape = (ang_r.shape[0],) + (1,) * (x.ndim - 3) + (ang_r.shape[1],)
    x32 = x.astype(jnp.float32)
    d = x.shape[-1]
    out = jnp.concatenate([rope_1d(x32[..., : d // 2], ang_r.reshape(shape)),
                           rope_1d(x32[..., d // 2:], ang_c.reshape(shape))], axis=-1)
    return out.astype(x.dtype)


def attend(q, k, v, scale):
    s = jnp.einsum('bqhd,bkhd->bhqk', q, k) * scale
    p = softmax_f32(s).astype(v.dtype)
    return jnp.einsum('bhqk,bkhe->bqhe', p, v)


def diff_attend(q, k, v, lam, scale):
    s = jnp.einsum('bqhnd,bkhnd->bnhqk', q, k) * scale
    p = softmax_f32(s)
    pd = (p[:, 0] - lam * p[:, 1]).astype(v.dtype)
    return jnp.einsum('bhqk,bkhe->bqhe', pd, v)


def sweep_query_blocks(fn, q):
    B, S = q.shape[:2]
    nb = S // Q_BLOCK
    qb = jnp.moveaxis(q.reshape((B, nb, Q_BLOCK) + q.shape[2:]), 1, 0)
    out = lax.map(fn, qb)
    return jnp.moveaxis(out, 0, 1).reshape((B, S) + out.shape[3:])


def neighbourhood_attend(q, k, v, k_ctx, v_ctx, rpb):
    B, S, H, d = q.shape
    L = k_ctx.shape[1]
    rows = S // GRID_W
    kh = min(NA_ROWS, rows)
    n_cb = GRID_W // NA_COLS
    span = 2 * NA_COLS
    qcols = np.arange(GRID_W).reshape(n_cb, NA_COLS)
    col_start = np.clip(np.arange(n_cb) * NA_COLS - NA_COLS // 2, 0, GRID_W - span)
    kcols = col_start[:, None] + np.arange(span)
    cs = np.clip(qcols - NA_COLS // 2, 0, GRID_W - NA_COLS)
    kc = kcols[:, None, :]
    col_ok = (kc >= cs[..., None]) & (kc < cs[..., None] + NA_COLS)
    dc = kc - qcols[..., None] + (NA_COLS - 1)
    mask = jnp.asarray(np.broadcast_to(col_ok[:, :, None, :], (n_cb, NA_COLS, kh, span))
                       .reshape(n_cb, NA_COLS, kh * span))
    kg = k.reshape(B, rows, GRID_W, H, d)
    vg = v.reshape(B, rows, GRID_W, H, d)
    qg = jnp.moveaxis(q.reshape(B, rows, n_cb, NA_COLS, H, d), 1, 0)

    def row_fn(args):
        r, q_row = args
        rs = jnp.clip(r - kh // 2, 0, rows - kh)
        k_rows = lax.dynamic_slice_in_dim(kg, rs, kh, axis=1)[:, :, kcols]
        v_rows = lax.dynamic_slice_in_dim(vg, rs, kh, axis=1)[:, :, kcols]
        kb = jnp.moveaxis(k_rows, 2, 1).reshape(B, n_cb, kh * span, H, d)
        vb = jnp.moveaxis(v_rows, 2, 1).reshape(B, n_cb, kh * span, H, d)
        dr = rs + jnp.arange(kh) - r + (NA_ROWS - 1)
        bias = rpb[:, dr[None, None, :, None], dc[:, :, None, :]]
        bias = bias.reshape(H, n_cb, NA_COLS, kh * span).astype(jnp.float32)
        s_lat = jnp.einsum('bjqhd,bjkhd->bhjqk', q_row, kb).astype(jnp.float32) * NA_SCALE + bias[None]
        s_lat = jnp.where(mask[None, None], s_lat, NEG_INF)
        s_ctx = jnp.einsum('bjqhd,bkhd->bhjqk', q_row, k_ctx).astype(jnp.float32) * NA_SCALE
        p = softmax_f32(jnp.concatenate([s_ctx, s_lat], axis=-1)).astype(v.dtype)
        return (jnp.einsum('bhjqk,bkhe->bjqhe', p[..., :L], v_ctx)
                + jnp.einsum('bhjqk,bjkhe->bjqhe', p[..., L:], vb))

    out = lax.map(row_fn, (jnp.arange(rows), qg))
    return jnp.moveaxis(out, 0, 1).reshape(B, S, H, d)


def window_attend(q, k, v, k_ctx, v_ctx, sink):
    B, S, H, d = q.shape
    kvh = k.shape[2]
    g = H // kvh
    L = k_ctx.shape[1]
    nb = S // WINDOW
    span = 3 * WINDOW
    pad = ((0, 0), (WINDOW, WINDOW), (0, 0), (0, 0))
    kp, vp = jnp.pad(k, pad), jnp.pad(v, pad)
    qb = jnp.moveaxis(q.reshape(B, nb, WINDOW, kvh, g, d), 1, 0)
    sink_col = jnp.broadcast_to(sink.astype(jnp.float32).reshape(1, kvh, g, 1, 1), (B, kvh, g, WINDOW, 1))

    def blk_fn(args):
        n, q_blk = args
        kb = lax.dynamic_slice_in_dim(kp, n * WINDOW, span, axis=1)
        vb = lax.dynamic_slice_in_dim(vp, n * WINDOW, span, axis=1)
        qpos = n * WINDOW + jnp.arange(WINDOW)
        kpos = n * WINDOW - WINDOW + jnp.arange(span)
        ok = ((jnp.abs(qpos[:, None] - kpos[None, :]) <= WINDOW)
              & (kpos >= 0)[None, :] & (kpos < S)[None, :])
        s_lat = jnp.einsum('bqgrd,bkgd->bgrqk', q_blk, kb).astype(jnp.float32) * GQA_SCALE
        s_lat = jnp.where(ok, s_lat, NEG_INF)
        s_ctx = jnp.einsum('bqgrd,bkgd->bgrqk', q_blk, k_ctx).astype(jnp.float32) * GQA_SCALE
        p = softmax_f32(jnp.concatenate([sink_col, s_ctx, s_lat], axis=-1)).astype(v.dtype)
        o = (jnp.einsum('bgrqk,bkge->bqgre', p[..., 1:1 + L], v_ctx)
             + jnp.einsum('bgrqk,bkge->bqgre', p[..., 1 + L:], vb))
        return o.reshape(B, WINDOW, H, d)

    out = lax.map(blk_fn, (jnp.arange(nb), qb))
    return jnp.moveaxis(out, 0, 1).reshape(B, S, H, d)


def sink_attend(q, k, v, sink):
    B, L, H, d = q.shape
    kvh = k.shape[2]
    g = H // kvh
    qg = q.reshape(B, L, kvh, g, d)
    s = jnp.einsum('bqgrd,bkgd->bgrqk', qg, k).astype(jnp.float32) * GQA_SCALE
    sink_col = jnp.broadcast_to(sink.astype(jnp.float32).reshape(1, kvh, g, 1, 1), (B, kvh, g, L, 1))
    p = softmax_f32(jnp.concatenate([sink_col, s], axis=-1)).astype(v.dtype)
    return jnp.einsum('bgrqk,bkge->bqgre', p[..., 1:], v).reshape(B, L, H, d)


def split_cols(p):
    offs = np.cumsum(IN_SIZES)[:-1].tolist()
    return jnp.split(p, offs, axis=-1)


def mixer_inputs(p, rope, mla_q_norm, mla_w_uq, mla_kv_norm, mla_w_ukv):
    cq, ckv, kr, dq, dk, dv, nq, nk, nv, gq, gk, gv = split_cols(p)
    lead = p.shape[:-1]

    def rot(x, i):
        return x if rope is None else rope_2d(x, rope[i])

    qa = (rmsnorm(cq, mla_q_norm) @ mla_w_uq).reshape(lead + (MLA_HEADS, MLA_NOPE + MLA_ROPE))
    qa = jnp.concatenate([qa[..., :MLA_NOPE], rot(qa[..., MLA_NOPE:], 0)], axis=-1)
    kva = (rmsnorm(ckv, mla_kv_norm) @ mla_w_ukv).reshape(lead + (MLA_HEADS, MLA_NOPE + MLA_V))
    k_rope = rot(kr[..., None, :], 0)
    ka = jnp.concatenate([kva[..., :MLA_NOPE],
                          jnp.broadcast_to(k_rope, lead + (MLA_HEADS, MLA_ROPE))], axis=-1)
    va = kva[..., MLA_NOPE:]
    qb = rot(dq.reshape(lead + (DIFF_HEADS, 2, DIFF_HEAD)), 0)
    kb = rot(dk.reshape(lead + (DIFF_HEADS, 2, DIFF_HEAD)), 0)
    vb = dv.reshape(lead + (DIFF_HEADS, DIFF_V))
    qc = nq.reshape(lead + (NA_HEADS, NA_HEAD))
    kc = nk.reshape(lead + (NA_HEADS, NA_HEAD))
    vc = nv.reshape(lead + (NA_HEADS, NA_HEAD))
    qd = rot(gq.reshape(lead + (GQA_HEADS, GQA_HEAD)), 1)
    kd = rot(gk.reshape(lead + (GQA_KV_HEADS, GQA_HEAD)), 1)
    vd = gv.reshape(lead + (GQA_KV_HEADS, GQA_HEAD))
    return (qa, ka, va, qb, kb, vb, qc, kc, vc, qd, kd, vd)


def gated_merge(u, ys, w_branch, w_gate, b_gate, w_out):
    terms = []
    for i, y in enumerate(ys):
        y = y.reshape(y.shape[:2] + (BRANCH_W,))
        terms.append(jax.nn.sigmoid(u @ w_gate[i] + b_gate[i]) * (y @ w_branch[i]))
    merged = terms[0]
    for t in terms[1:]:
        merged = merged + t
    return merged @ w_out


def token_mix(u_lat, u_ctx, rope, lam_init, w_in, mla_q_norm, mla_w_uq, mla_kv_norm, mla_w_ukv,
              diff_lam, diff_subln, na_rpb, gqa_sink, w_branch, w_gate, b_gate, w_out, ctx_out):
    qa, ka, va, qb, kb, vb, qc, kc, vc, qd, kd, vd = mixer_inputs(
        u_lat @ w_in, rope, mla_q_norm, mla_w_uq, mla_kv_norm, mla_w_ukv)
    qa_c, ka_c, va_c, qb_c, kb_c, vb_c, qc_c, kc_c, vc_c, qd_c, kd_c, vd_c = mixer_inputs(
        u_ctx @ w_in, None, mla_q_norm, mla_w_uq, mla_kv_norm, mla_w_ukv)
    dl = diff_lam.astype(jnp.float32)
    lam = jnp.exp(jnp.sum(dl[0] * dl[1])) - jnp.exp(jnp.sum(dl[2] * dl[3])) + lam_init

    ka_all = jnp.concatenate([ka_c, ka], axis=1)
    va_all = jnp.concatenate([va_c, va], axis=1)
    ya = sweep_query_blocks(lambda qblk: attend(qblk, ka_all, va_all, MLA_SCALE), qa)
    kb_all = jnp.concatenate([kb_c, kb], axis=1)
    vb_all = jnp.concatenate([vb_c, vb], axis=1)
    yb = sweep_query_blocks(lambda qblk: diff_attend(qblk, kb_all, vb_all, lam, DIFF_SCALE), qb)
    yb = rmsnorm(yb, diff_subln, SUBLN_EPS) * (1 - lam_init)
    yc = neighbourhood_attend(qc, kc, vc, kc_c, vc_c, na_rpb)
    yd = window_attend(qd, kd, vd, kd_c, vd_c, gqa_sink)
    y_lat = gated_merge(u_lat, (ya, yb, yc, yd), w_branch, w_gate, b_gate, w_out)
    if not ctx_out:
        return y_lat, None
    ya_c = attend(qa_c, ka_c, va_c, MLA_SCALE)
    yb_c = rmsnorm(diff_attend(qb_c, kb_c, vb_c, lam, DIFF_SCALE), diff_subln, SUBLN_EPS) * (1 - lam_init)
    yc_c = attend(qc_c, kc_c, vc_c, NA_SCALE)
    yd_c = sink_attend(qd_c, kd_c, vd_c, gqa_sink)
    y_ctx = gated_merge(u_ctx, (ya_c, yb_c, yc_c, yd_c), w_branch, w_gate, b_gate, w_out)
    return y_lat, y_ctx


def setup_inputs(seed: int = 0) -> dict:
    key = jax.random.key(seed)
    ks = jax.random.split(key, 32)
    f32 = jnp.float32

    def nrm(k, shape, scale):
        return jax.random.normal(k, shape, f32) * scale

    def gain(k, shape):
        return 1.0 + nrm(k, shape, 0.05)

    D = D_MODEL
    return {
        "x": nrm(ks[0], (BATCH, SEQ, D), 1.0),
        "c": nrm(ks[1], (BATCH, D), 1.0),
        "ctx": nrm(ks[2], (BATCH, CTX_LEN, D), 1.0),
        "c_ctx": nrm(ks[3], (D,), 1.0),
        "w_ada": nrm(ks[4], (DEPTH, D, N_MOD * D), 0.5 * D ** -0.5),
        "b_ada": nrm(ks[5], (DEPTH, N_MOD * D), 0.01),
        "norm_ffn1": gain(ks[6], (DEPTH, D)),
        "ffn1_w_gu": nrm(ks[7], (DEPTH, D, 2 * D_FF), D ** -0.5),
        "ffn1_w_down": nrm(ks[8], (DEPTH, D_FF, D), D_FF ** -0.5),
        "norm_mix": gain(ks[9], (DEPTH, D)),
        "w_in": nrm(ks[10], (DEPTH, D, IN_COLS), D ** -0.5),
        "mla_q_norm": gain(ks[11], (DEPTH, MLA_Q_RANK)),
        "mla_w_uq": nrm(ks[12], (DEPTH, MLA_Q_RANK, MLA_HEADS * (MLA_NOPE + MLA_ROPE)), MLA_Q_RANK ** -0.5),
        "mla_kv_norm": gain(ks[13], (DEPTH, MLA_KV_RANK)),
        "mla_w_ukv": nrm(ks[14], (DEPTH, MLA_KV_RANK, MLA_HEADS * (MLA_NOPE + MLA_V)), MLA_KV_RANK ** -0.5),
        "diff_lam": nrm(ks[15], (DEPTH, 4, DIFF_HEAD), 0.1),
        "diff_subln": gain(ks[16], (DEPTH, DIFF_V)),
        "na_rpb": nrm(ks[17], (DEPTH, NA_HEADS, 2 * NA_ROWS - 1, 2 * NA_COLS - 1), 0.1),
        "gqa_sink": nrm(ks[18], (DEPTH, GQA_HEADS), 0.5),
        "w_branch": nrm(ks[19], (DEPTH, N_BRANCH, BRANCH_W, D), BRANCH_W ** -0.5),
        "w_gate": nrm(ks[20], (DEPTH, N_BRANCH, D, D), D ** -0.5),
        "b_gate": nrm(ks[21], (DEPTH, N_BRANCH, D), 0.01),
        "w_out": nrm(ks[22], (DEPTH, D, D), D ** -0.5),
        "norm_ffn2": gain(ks[23], (DEPTH, D)),
        "ffn2_w_gu": nrm(ks[24], (DEPTH, D, 2 * D_FF), D ** -0.5),
        "ffn2_w_down": nrm(ks[25], (DEPTH, D_FF, D), D_FF ** -0.5),
        "final_norm": gain(ks[26], (D,)),
    }


def reference(x, c, ctx, c_ctx, w_ada, b_ada, norm_ffn1, ffn1_w_gu, ffn1_w_down, norm_mix, w_in,
              mla_q_norm, mla_w_uq, mla_kv_norm, mla_w_ukv, diff_lam, diff_subln, na_rpb, gqa_sink,
              w_branch, w_gate, b_gate, w_out, norm_ffn2, ffn2_w_gu, ffn2_w_down, final_norm):
    B, S, D = x.shape
    rope = (axial_angles(S, MLA_ROPE), axial_angles(S, GQA_HEAD))
    s_c = jax.nn.silu(c)
    s_cc = jax.nn.silu(c_ctx)
    h, hc = x, ctx
    for l in range(DEPTH):
        last = l == DEPTH - 1
        lam_init = 0.8 - 0.6 * math.exp(-0.3 * l)
        mod = (s_c @ w_ada[l] + b_ada[l]).reshape(B, N_MOD, 1, D)
        mod_c = (s_cc @ w_ada[l] + b_ada[l]).reshape(N_MOD, D)
        h = h + 0.5 * mod[:, 2] * swiglu(modulate(h, norm_ffn1[l], mod[:, 0], mod[:, 1]),
                                         ffn1_w_gu[l], ffn1_w_down[l])
        hc = hc + 0.5 * mod_c[2] * swiglu(modulate(hc, norm_ffn1[l], mod_c[0], mod_c[1]),
                                          ffn1_w_gu[l], ffn1_w_down[l])
        u = modulate(h, norm_mix[l], mod[:, 3], mod[:, 4])
        uc = modulate(hc, norm_mix[l], mod_c[3], mod_c[4])
        y, yc = token_mix(u, uc, rope, lam_init, w_in[l], mla_q_norm[l], mla_w_uq[l], mla_kv_norm[l],
                          mla_w_ukv[l], diff_lam[l], diff_subln[l], na_rpb[l], gqa_sink[l],
                          w_branch[l], w_gate[l], b_gate[l], w_out[l], not last)
        h = h + mod[:, 5] * y
        h = h + 0.5 * mod[:, 8] * swiglu(modulate(h, norm_ffn2[l], mod[:, 6], mod[:, 7]),
                                         ffn2_w_gu[l], ffn2_w_down[l])
        if not last:
            hc = hc + mod_c[5] * yc
            hc = hc + 0.5 * mod_c[8] * swiglu(modulate(hc, norm_ffn2[l], mod_c[6], mod_c[7]),
                                              ffn2_w_gu[l], ffn2_w_down[l])
    return rmsnorm(h, final_norm)
```

```python
import functools
import math

import numpy as np
import jax
import jax.numpy as jnp
from jax import lax
from jax.experimental import pallas as pl
from jax.experimental.pallas import tpu as pltpu

F32 = jnp.float32
BF16 = jnp.bfloat16

GRID_W = 64
N_MOD = 9
ROPE_BASE = 10000.0
NEG_INF = -1e30
EPS = 1e-6
SUBLN_EPS = 1e-5
N_HEADS = 4
HEAD_V = 64
MLA_Q_RANK = 256
MLA_KV_RANK = 128
MLA_NOPE = 64
MLA_ROPE = 32
MLA_SCALE = (MLA_NOPE + MLA_ROPE) ** -0.5
DIFF_HEAD = 32
DIFF_SCALE = DIFF_HEAD ** -0.5
NA_HEAD = 64
NA_ROWS = 8
NA_COLS = 16
NA_SCALE = NA_HEAD ** -0.5
GQA_KV_HEADS = 2
GQA_HEAD = 64
WINDOW = 128
GQA_SCALE = GQA_HEAD ** -0.5
BRANCH_W = 256
IN_SIZES = (256, 128, 32, 256, 256, 256, 256, 256, 256, 256, 128, 128)

LANES = 128
TQ = 256
TK = 256
VROWS = 80
ONES_ROW = HEAD_V
TM = 512
FF_CHUNK = 256
VMEM_LIMIT = 56 * 1024 * 1024


def _cparams(n_axes):
    return pltpu.CompilerParams(dimension_semantics=("arbitrary",) * n_axes,
                                vmem_limit_bytes=VMEM_LIMIT)


def _rms(x, eps):
    return x * lax.rsqrt(jnp.mean(x * x, axis=-1, keepdims=True) + eps)


def _modulate(h, g, shift, scale):
    return (_rms(h, EPS) * g) * (1.0 + scale) + shift


def _nt_dot(a, b):
    return lax.dot_general(a, b, (((1,), (1,)), ((), ())), preferred_element_type=F32)


def _mod_kernel(s_ref, w_ref, b_ref, o_ref):
    s = s_ref[...]
    s = s * jax.nn.sigmoid(s)
    o_ref[0] = jnp.dot(s, w_ref[0], preferred_element_type=F32,
                       precision=lax.Precision.HIGHEST) + b_ref[0]


def _modulation(cc, w_ada, b_ada):
    depth, d, nd = w_ada.shape
    r = cc.shape[0]
    return pl.pallas_call(
        _mod_kernel,
        grid=(depth, nd // d),
        in_specs=[pl.BlockSpec((r, d), lambda l, n: (0, 0)),
                  pl.BlockSpec((1, d, d), lambda l, n: (l, 0, n)),
                  pl.BlockSpec((1, 1, d), lambda l, n: (l, 0, n))],
        out_specs=pl.BlockSpec((1, r, d), lambda l, n: (l, 0, n)),
        out_shape=jax.ShapeDtypeStruct((depth, r, nd), F32),
        compiler_params=_cparams(2),
        name="adaln_mod",
    )(cc, w_ada, b_ada.reshape(depth, 1, nd))


def _ffn_kernel(*refs, mod_base, final):
    if final:
        h_ref, mod_ref, g_ref, wgu_ref, wdn_ref, fn_ref, o_ref, a_scr = refs
    else:
        h_ref, mod_ref, g_ref, wgu_ref, wdn_ref, o_ref, a_scr = refs
    h = h_ref[...]
    mod = mod_ref[0]
    shift = mod[mod_base:mod_base + 1]
    scale = mod[mod_base + 1:mod_base + 2]
    gate = mod[mod_base + 2:mod_base + 3]
    u = _modulate(h, g_ref[...], shift, scale).astype(BF16)
    for j in range(wgu_ref.shape[0]):
        gu = jnp.dot(u, wgu_ref[j], preferred_element_type=F32)
        g = gu[:, :FF_CHUNK]
        up = gu[:, FF_CHUNK:]
        a_scr[:, j * FF_CHUNK:(j + 1) * FF_CHUNK] = (g * jax.nn.sigmoid(g) * up).astype(BF16)
    y = jnp.dot(a_scr[...], wdn_ref[...], preferred_element_type=F32)
    out = h + (0.5 * gate) * y
    if final:
        out = _rms(out, EPS) * fn_ref[...]
    o_ref[...] = out


def _ffn(h, mod_l, g, wgu_c, wdn, *, mod_base, n_tok_out, seq, n_batch, final_g=None):
    d = h.shape[1]
    n_chunks = wgu_c.shape[0]
    d_ff = wdn.shape[0]
    final = final_g is not None
    const2 = lambda i: (0, 0)
    in_specs = [pl.BlockSpec((TM, d), lambda i: (i, 0)),
                pl.BlockSpec((1, N_MOD, d), lambda i: (jnp.minimum(i * TM // seq, n_batch), 0, 0)),
                pl.BlockSpec((1, d), const2),
                pl.BlockSpec((n_chunks, d, 2 * FF_CHUNK), lambda i: (0, 0, 0)),
                pl.BlockSpec((d_ff, d), const2)]
    args = [h, mod_l, g, wgu_c, wdn]
    if final:
        in_specs.append(pl.BlockSpec((1, d), const2))
        args.append(final_g)
    return pl.pallas_call(
        functools.partial(_ffn_kernel, mod_base=mod_base, final=final),
        grid=(n_tok_out // TM,),
        in_specs=in_specs,
        out_specs=pl.BlockSpec((TM, d), lambda i: (i, 0)),
        out_shape=jax.ShapeDtypeStruct((n_tok_out, d), F32),
        scratch_shapes=[pltpu.VMEM((TM, d_ff), BF16)],
        compiler_params=_cparams(1),
        name="ffn",
    )(*args)


def _rope(x, cos, sgn_sin, half):
    lane = lax.broadcasted_iota(jnp.int32, (1, LANES), 1)
    first = (lane & (2 * half - 1)) < half
    outs = []
    for g in range(x.shape[1] // LANES):
        xg = x[:, g * LANES:(g + 1) * LANES]
        partner = jnp.where(first, pltpu.roll(xg, LANES - half, 1), pltpu.roll(xg, half, 1))
        outs.append(xg * cos + partner * sgn_sin)
    return outs


def _mixin_kernel(h_ref, mod_ref, g_ref, c32_ref, s32_ref, c64_ref, s64_ref,
                  wmain_ref, wt_ref, bt_ref, qn_ref, kvn_ref, wq2_ref, wk2_ref, wv2_ref, bv2_ref,
                  qa_ref, ka_ref, vat_ref, qb_ref, kb_ref, vbt_ref, qc_ref, kc_ref, vct_ref,
                  qd_ref, kd_ref, vdt_ref):
    mod = mod_ref[0]
    u = _modulate(h_ref[...], g_ref[...], mod[3:4], mod[4:5]).astype(BF16)
    pm = jnp.dot(u, wmain_ref[...], preferred_element_type=F32)
    c32, s32, c64, s64 = c32_ref[...], s32_ref[...], c64_ref[...], s64_ref[...]
    lane = lax.broadcasted_iota(jnp.int32, (1, LANES), 1)
    rope_lanes = (lane >= MLA_NOPE) & (lane < MLA_NOPE + MLA_ROPE)
    c_mla = jnp.where(rope_lanes, c32, 1.0)
    s_mla = jnp.where(rope_lanes, s32, 0.0)

    cqn = (_rms(pm[:, 0:256], EPS) * qn_ref[...]).astype(BF16)
    ckvn = (_rms(pm[:, 256:384], EPS) * kvn_ref[...]).astype(BF16)
    qa = _rope(jnp.dot(cqn, wq2_ref[...], preferred_element_type=F32), c_mla, s_mla, MLA_ROPE // 4)
    k_rope = _rope(pm[:, 384:512], c_mla, s_mla, MLA_ROPE // 4)[0]
    ka = jnp.dot(ckvn, wk2_ref[...], preferred_element_type=F32)
    for hd in range(N_HEADS):
        sl = slice(hd * LANES, (hd + 1) * LANES)
        qa_ref[:, sl] = qa[hd].astype(BF16)
        ka_ref[:, sl] = (ka[:, sl] + k_rope).astype(BF16)
    vat = _nt_dot(wv2_ref[...], ckvn) + bv2_ref[...]

    for idx, (ref, lo, cos, sin, half) in enumerate((
            (qb_ref, 512, c32, s32, DIFF_HEAD // 4), (kb_ref, 768, c32, s32, DIFF_HEAD // 4),
            (qd_ref, 1536, c64, s64, GQA_HEAD // 4), (kd_ref, 1792, c64, s64, GQA_HEAD // 4))):
        parts = _rope(pm[:, lo:lo + 256], cos, sin, half)
        for g in range(2):
            ref[:, g * LANES:(g + 1) * LANES] = parts[g].astype(BF16)
    qc_ref[...] = pm[:, 1024:1280].astype(BF16)
    kc_ref[...] = pm[:, 1280:1536].astype(BF16)

    vt = _nt_dot(wt_ref[...], u) + bt_ref[...]
    nv = N_HEADS * VROWS
    for c in range(TM // TK):
        cs = slice(c * TK, (c + 1) * TK)
        vat_ref[c] = vat[:, cs].astype(BF16)
        vbt_ref[c] = vt[0:nv, cs].astype(BF16)
        vct_ref[c] = vt[nv:2 * nv, cs].astype(BF16)
        vdt_ref[c] = vt[2 * nv:, cs].astype(BF16)


def _mixin(h, mod_l, g, tabs, w, *, seq, n_batch):
    t, d = h.shape
    n_lat_tiles = n_batch * seq // TM
    tiles_per_seq = seq // TM
    tok = lambda i: (i, 0)
    const2 = lambda i: (0, 0)
    tab_map = lambda i: (jnp.where(i < n_lat_tiles, i % tiles_per_seq, tiles_per_seq), 0)
    vt_map = lambda i: (i, 0, 0)
    nv = N_HEADS * VROWS
    nvd = GQA_KV_HEADS * VROWS
    full = lambda a: pl.BlockSpec(a.shape, const2)
    in_specs = [pl.BlockSpec((TM, d), tok),
                pl.BlockSpec((1, N_MOD, d), lambda i: (jnp.minimum(i * TM // seq, n_batch), 0, 0)),
                pl.BlockSpec((1, d), const2)]
    in_specs += [pl.BlockSpec((TM, LANES), tab_map)] * 4
    wlist = [w["wmain"], w["wt"], w["bt"], w["qn"], w["kvn"], w["wq2"], w["wk2"], w["wv2"], w["bv2"]]
    in_specs += [full(a) for a in wlist]
    nat = lambda width: (pl.BlockSpec((TM, width), tok), jax.ShapeDtypeStruct((t, width), BF16))
    tr = lambda rows: (pl.BlockSpec((TM // TK, rows, TK), vt_map),
                       jax.ShapeDtypeStruct((t // TK, rows, TK), BF16))
    outs = [nat(512), nat(512), tr(nv), nat(256), nat(256), tr(nv), nat(256), nat(256), tr(nv),
            nat(256), nat(256), tr(nvd)]
    return pl.pallas_call(
        _mixin_kernel,
        grid=(t // TM,),
        in_specs=in_specs,
        out_specs=[o[0] for o in outs],
        out_shape=[o[1] for o in outs],
        compiler_params=_cparams(1),
        name="mixer_in",
    )(h, mod_l, g, *tabs, *wlist)


def _lane_masked(qg, lo, width):
    lane = lax.broadcasted_iota(jnp.int32, (1, LANES), 1)
    return jnp.where((lane >= lo) & (lane < lo + width), qg, jnp.zeros_like(qg))


def _attn_tile(kt, vt, qm, state, keep=None, bias=None):
    m, acc = state
    s = _nt_dot(kt, qm)
    if bias is not None:
        s = s + bias
    if keep is not None:
        s = jnp.where(keep, s, NEG_INF)
    m_new = jnp.maximum(m, jnp.max(s, axis=0, keepdims=True))
    p = jnp.exp(s - m_new).astype(BF16)
    acc = acc * jnp.exp(m - m_new) + jnp.dot(vt, p, preferred_element_type=F32)
    return m_new, acc


def _init_state(m0=None):
    if m0 is None:
        return jnp.full((1, TQ), NEG_INF, F32), jnp.zeros((VROWS, TQ), F32)
    row = lax.broadcasted_iota(jnp.int32, (VROWS, TQ), 0)
    return jnp.full((1, TQ), m0, F32), jnp.where(row == ONES_ROW, 1.0, 0.0).astype(F32)


def _normalised(state):
    _, acc = state
    return acc[0:HEAD_V] / acc[ONES_ROW:ONES_ROW + 1]


def _q_block_map(nq, n_batch):
    return lambda b, j: (jnp.where(j < nq, b * nq + j, n_batch * nq + b), 0)


def _dense_attn_kernel(*refs, diff, nq, lam_init):
    if diff:
        lamp_ref, subln_ref, q_ref, kc_ref, kl_ref, vc_ref, vl_ref, o_ref, ot_scr = refs
    else:
        q_ref, kc_ref, kl_ref, vc_ref, vl_ref, o_ref, ot_scr = refs
    j = pl.program_id(1)
    n_lat = jnp.where(j < nq, nq, 0)
    if diff:
        dl = lamp_ref[...]
        lam = (jnp.exp(jnp.sum(dl[0:1] * dl[1:2], axis=1, keepdims=True))
               - jnp.exp(jnp.sum(dl[2:3] * dl[3:4], axis=1, keepdims=True)) + lam_init)
    for hd in range(N_HEADS):
        if diff:
            grp = hd // 2
            qg = q_ref[:, grp * LANES:(grp + 1) * LANES]
            qs = [_lane_masked(qg, 2 * DIFF_HEAD * (hd % 2) + DIFF_HEAD * n, DIFF_HEAD) for n in range(2)]
        else:
            grp = hd
            qs = [q_ref[:, grp * LANES:(grp + 1) * LANES]]
        kcols = slice(grp * LANES, (grp + 1) * LANES)
        vrows = slice(hd * VROWS, (hd + 1) * VROWS)

        def step(kt, vt, states):
            return tuple(_attn_tile(kt, vt, qm, st) for qm, st in zip(qs, states))

        states = step(kc_ref[:, kcols], vc_ref[0, vrows, :], tuple(_init_state() for _ in qs))

        def body(t, states):
            off = pl.multiple_of(t * TK, TK)
            return step(kl_ref[pl.ds(off, TK), kcols], vl_ref[t, vrows, :], states)

        states = lax.fori_loop(0, n_lat, body, states)
        if diff:
            o = _normalised(states[0]) - lam * _normalised(states[1])
            o = o * lax.rsqrt(jnp.mean(o * o, axis=0, keepdims=True) + SUBLN_EPS)
            o = o * subln_ref[...] * (1.0 - lam_init)
        else:
            o = _normalised(states[0])
        ot_scr[hd * HEAD_V:(hd + 1) * HEAD_V, :] = o
    o_ref[...] = ot_scr[...].T.astype(BF16)


def _dense_attn(q, k, vt, *, seq, n_batch, diff, lam_init=0.0, lamp=None, subln=None):
    t, qw = q.shape
    kw = k.shape[1]
    nq = seq // TQ
    n_lat_blocks = n_batch * nq
    nv = vt.shape[1]
    qmap = _q_block_map(nq, n_batch)
    in_specs = [pl.BlockSpec((TQ, qw), qmap),
                pl.BlockSpec((TK, kw), lambda b, j: (n_lat_blocks + b, 0)),
                pl.BlockSpec((seq, kw), lambda b, j: (b, 0)),
                pl.BlockSpec((1, nv, TK), lambda b, j: (n_lat_blocks + b, 0, 0)),
                pl.BlockSpec((nq, nv, TK), lambda b, j: (b, 0, 0))]
    args = [q, k, k, vt, vt]
    if diff:
        in_specs = [pl.BlockSpec(lamp.shape, lambda b, j: (0, 0)),
                    pl.BlockSpec(subln.shape, lambda b, j: (0, 0))] + in_specs
        args = [lamp, subln] + args
    return pl.pallas_call(
        functools.partial(_dense_attn_kernel, diff=diff, nq=nq, lam_init=lam_init),
        grid=(n_batch, nq + 1),
        in_specs=in_specs,
        out_specs=pl.BlockSpec((TQ, BRANCH_W), qmap),
        out_shape=jax.ShapeDtypeStruct((t, BRANCH_W), BF16),
        scratch_shapes=[pltpu.VMEM((BRANCH_W, TQ), F32)],
        compiler_params=_cparams(2),
        name="diff_attn" if diff else "mla_attn",
    )(*args)


NA_Q_ROWS = TQ // GRID_W
NA_K_TILES = 3


def _na_attn_kernel(q_ref, kc_ref, k0_ref, k1_ref, k2_ref, vc_ref, v0_ref, v1_ref, v2_ref, bias_ref,
                    o_ref, ot_scr):
    k_refs = (k0_ref, k1_ref, k2_ref)
    v_refs = (v0_ref, v1_ref, v2_ref)
    for hd in range(N_HEADS):
        grp = hd // 2
        kcols = slice(grp * LANES, (grp + 1) * LANES)
        vrows = slice(hd * VROWS, (hd + 1) * VROWS)
        qm = _lane_masked(q_ref[:, kcols], NA_HEAD * (hd % 2), NA_HEAD)
        state = _attn_tile(kc_ref[:, kcols], vc_ref[0, vrows, :], qm, _init_state())
        for t in range(NA_K_TILES):
            state = _attn_tile(k_refs[t][:, kcols], v_refs[t][0, vrows, :], qm, state,
                               bias=bias_ref[0, hd, t * TK:(t + 1) * TK, :])
        ot_scr[hd * HEAD_V:(hd + 1) * HEAD_V, :] = _normalised(state)
    o_ref[...] = ot_scr[...].T.astype(BF16)


def _na_bias_tables(rpb, rows):
    kk = np.arange(NA_K_TILES * TK)
    qq = np.arange(TQ)
    tabs = []
    for r0 in (0, NA_Q_ROWS, rows - NA_Q_ROWS):
        base = int(np.clip(r0 - NA_ROWS // 2, 0, rows - NA_K_TILES * NA_Q_ROWS))
        kr = (base + kk // GRID_W)[:, None]
        kc = (kk % GRID_W)[:, None]
        r = (r0 + qq // GRID_W)[None, :]
        c = (qq % GRID_W)[None, :]
        rs = np.clip(r - NA_ROWS // 2, 0, rows - NA_ROWS)
        cs = np.clip(c - NA_COLS // 2, 0, GRID_W - NA_COLS)
        ok = (kr >= rs) & (kr < rs + NA_ROWS) & (kc >= cs) & (kc < cs + NA_COLS)
        dr = np.clip(kr - r + NA_ROWS - 1, 0, 2 * NA_ROWS - 2)
        dc = np.clip(kc - c + NA_COLS - 1, 0, 2 * NA_COLS - 2)
        tabs.append(jnp.where(jnp.asarray(ok)[None], rpb[:, dr, dc].astype(F32), NEG_INF))
    tabs.append(jnp.full_like(tabs[0], NEG_INF))
    return jnp.stack(tabs)


def _na_attn(q, k, vt, bias, *, seq, n_batch):
    t, w = q.shape
    nq = seq // TQ
    n_lat_blocks = n_batch * nq
    nv = vt.shape[1]
    qmap = _q_block_map(nq, n_batch)

    def kblock(tile):
        return lambda b, j: (b * nq + jnp.clip(j - 1, 0, nq - NA_K_TILES) + tile, 0)

    def vblock(tile):
        return lambda b, j: (b * nq + jnp.clip(j - 1, 0, nq - NA_K_TILES) + tile, 0, 0)

    variant = lambda b, j: (jnp.where(j == 0, 0, jnp.where(j == nq - 1, 2, jnp.where(j == nq, 3, 1))),
                            0, 0, 0)
    in_specs = ([pl.BlockSpec((TQ, w), qmap),
                 pl.BlockSpec((TK, w), lambda b, j: (n_lat_blocks + b, 0))]
                + [pl.BlockSpec((TK, w), kblock(i)) for i in range(NA_K_TILES)]
                + [pl.BlockSpec((1, nv, TK), lambda b, j: (n_lat_blocks + b, 0, 0))]
                + [pl.BlockSpec((1, nv, TK), vblock(i)) for i in range(NA_K_TILES)]
                + [pl.BlockSpec((1,) + bias.shape[1:], variant)])
    return pl.pallas_call(
        _na_attn_kernel,
        grid=(n_batch, nq + 1),
        in_specs=in_specs,
        out_specs=pl.BlockSpec((TQ, BRANCH_W), qmap),
        out_shape=jax.ShapeDtypeStruct((t, BRANCH_W), BF16),
        scratch_shapes=[pltpu.VMEM((BRANCH_W, TQ), F32)],
        compiler_params=_cparams(2),
        name="na_attn",
    )(q, k, k, k, k, vt, vt, vt, vt, bias)


def _gqa_attn_kernel(sink_ref, q_ref, kc_ref, k0_ref, k1_ref, k2_ref, vc_ref, v0_ref, v1_ref, v2_ref,
                     o_ref, ot_scr, *, nq):
    j = pl.program_id(1)
    k_refs = (k0_ref, k1_ref, k2_ref)
    v_refs = (v0_ref, v1_ref, v2_ref)
    key_i = lax.broadcasted_iota(jnp.int32, (TK, TQ), 0)
    qry_i = lax.broadcasted_iota(jnp.int32, (TK, TQ), 1)
    keeps = []
    for rel in (-1, 0, 1):
        tile = j + rel
        tile_ok = (tile >= 0) & (tile < nq) & (j < nq)
        reach = jnp.where(tile_ok, WINDOW, -1)
        keeps.append(jnp.abs(rel * TK + key_i - qry_i) <= reach)
    for hd in range(N_HEADS):
        kv = hd // (N_HEADS // GQA_KV_HEADS)
        kcols = slice(kv * LANES, (kv + 1) * LANES)
        vrows = slice(kv * VROWS, (kv + 1) * VROWS)
        qm = _lane_masked(q_ref[:, kcols], GQA_HEAD * (hd % 2), GQA_HEAD)
        state = _attn_tile(kc_ref[:, kcols], vc_ref[0, vrows, :], qm, _init_state(sink_ref[hd]))
        for t in range(3):
            state = _attn_tile(k_refs[t][:, kcols], v_refs[t][0, vrows, :], qm, state, keep=keeps[t])
        ot_scr[hd * HEAD_V:(hd + 1) * HEAD_V, :] = _normalised(state)
    o_ref[...] = ot_scr[...].T.astype(BF16)


def _gqa_attn(q, k, vt, sink, *, seq, n_batch):
    t, w = q.shape
    nq = seq // TQ
    n_lat_blocks = n_batch * nq
    nv = vt.shape[1]
    qmap = _q_block_map(nq, n_batch)

    def kblock(rel):
        return lambda b, j: (b * nq + jnp.clip(j + rel, 0, nq - 1), 0)

    def vblock(rel):
        return lambda b, j: (b * nq + jnp.clip(j + rel, 0, nq - 1), 0, 0)

    in_specs = ([pl.BlockSpec(memory_space=pltpu.SMEM),
                 pl.BlockSpec((TQ, w), qmap),
                 pl.BlockSpec((TK, w), lambda b, j: (n_lat_blocks + b, 0))]
                + [pl.BlockSpec((TK, w), kblock(rel)) for rel in (-1, 0, 1)]
                + [pl.BlockSpec((1, nv, TK), lambda b, j: (n_lat_blocks + b, 0, 0))]
                + [pl.BlockSpec((1, nv, TK), vblock(rel)) for rel in (-1, 0, 1)])
    return pl.pallas_call(
        functools.partial(_gqa_attn_kernel, nq=nq),
        grid=(n_batch, nq + 1),
        in_specs=in_specs,
        out_specs=pl.BlockSpec((TQ, BRANCH_W), qmap),
        out_shape=jax.ShapeDtypeStruct((t, BRANCH_W), BF16),
        scratch_shapes=[pltpu.VMEM((BRANCH_W, TQ), F32)],
        compiler_params=_cparams(2),
        name="gqa_attn",
    )(sink, q, k, k, k, k, vt, vt, vt, vt)


def _merge_kernel(h_ref, mod_ref, g_ref, ya_ref, yb_ref, yc_ref, yd_ref, wg_ref, bg_ref, wb_ref, wo_ref,
                  o_ref):
    h = h_ref[...]
    mod = mod_ref[0]
    u = _modulate(h, g_ref[...], mod[3:4], mod[4:5]).astype(BF16)
    merged = None
    for i, y_ref in enumerate((ya_ref, yb_ref, yc_ref, yd_ref)):
        gate = jax.nn.sigmoid(jnp.dot(u, wg_ref[i], preferred_element_type=F32) + bg_ref[i])
        term = gate * jnp.dot(y_ref[...], wb_ref[i], preferred_element_type=F32)
        merged = term if merged is None else merged + term
    y = jnp.dot(merged.astype(BF16), wo_ref[...], preferred_element_type=F32)
    o_ref[...] = h + mod[5:6] * y


def _merge(h, mod_l, g, ys, wg, bg, wb, wo, *, seq, n_batch):
    t, d = h.shape
    tok = lambda i: (i, 0)
    const2 = lambda i: (0, 0)
    const3 = lambda i: (0, 0, 0)
    in_specs = ([pl.BlockSpec((TM, d), tok),
                 pl.BlockSpec((1, N_MOD, d), lambda i: (jnp.minimum(i * TM // seq, n_batch), 0, 0)),
                 pl.BlockSpec((1, d), const2)]
                + [pl.BlockSpec((TM, BRANCH_W), tok)] * 4
                + [pl.BlockSpec(wg.shape, const3), pl.BlockSpec(bg.shape, const3),
                   pl.BlockSpec(wb.shape, const3), pl.BlockSpec(wo.shape, const2)])
    return pl.pallas_call(
        _merge_kernel,
        grid=(t // TM,),
        in_specs=in_specs,
        out_specs=pl.BlockSpec((TM, d), tok),
        out_shape=jax.ShapeDtypeStruct((t, d), F32),
        compiler_params=_cparams(1),
        name="merge",
    )(h, mod_l, g, *ys, wg, bg, wb, wo)


def _rope_tables(seq, dim):
    tpos = jnp.arange(seq)
    rows = (tpos // GRID_W).astype(F32)
    cols = (tpos % GRID_W).astype(F32)
    half = dim // 2
    freqs = jnp.power(ROPE_BASE, -jnp.arange(0, half, 2, dtype=F32) / half)
    ang = jnp.concatenate([rows[:, None] * freqs] * 2 + [cols[:, None] * freqs] * 2, axis=1)
    nfreq = dim // 4
    sign = np.where((np.arange(dim) % (2 * nfreq)) < nfreq, -1.0, 1.0).astype(np.float32)
    reps = LANES // dim
    cos = jnp.tile(jnp.cos(ang), (1, reps))
    sin = jnp.tile(jnp.sin(ang) * sign, (1, reps))
    cos = jnp.concatenate([cos, jnp.ones((TM, LANES), F32)], axis=0)
    sin = jnp.concatenate([sin, jnp.zeros((TM, LANES), F32)], axis=0)
    return cos, sin


def _vt_weight(w_cols, n_heads):
    d_in = w_cols.shape[0]
    wt = w_cols.T.reshape(n_heads, HEAD_V, d_in)
    wt = jnp.pad(wt, ((0, 0), (0, VROWS - HEAD_V), (0, 0))).reshape(n_heads * VROWS, d_in)
    bias = np.zeros((n_heads, VROWS, 1), np.float32)
    bias[:, ONES_ROW] = 1.0
    return wt, jnp.asarray(bias.reshape(n_heads * VROWS, 1))


def _mixin_weights(w_in, q_norm, w_uq, kv_norm, w_ukv):
    offs = np.concatenate([[0], np.cumsum(IN_SIZES)])
    col = lambda i: w_in[:, offs[i]:offs[i + 1]]
    d = w_in.shape[0]
    kr = jnp.zeros((d, LANES), F32).at[:, MLA_NOPE:MLA_NOPE + MLA_ROPE].set(col(2))
    gk = col(10).reshape(d, GQA_KV_HEADS, 1, GQA_HEAD)
    gk2 = jnp.broadcast_to(gk, (d, GQA_KV_HEADS, 2, GQA_HEAD)).reshape(d, 2 * GQA_KV_HEADS * GQA_HEAD)
    wmain = jnp.concatenate([col(0), col(1), kr, col(3) * DIFF_SCALE, col(4), col(6) * NA_SCALE, col(7),
                             col(9) * GQA_SCALE, gk2], axis=1)
    wt_b, bt_b = _vt_weight(col(5), N_HEADS)
    wt_c, bt_c = _vt_weight(col(8), N_HEADS)
    wt_d, bt_d = _vt_weight(col(11), GQA_KV_HEADS)
    qk = MLA_NOPE + MLA_ROPE
    wq2 = jnp.pad((w_uq * MLA_SCALE).reshape(MLA_Q_RANK, N_HEADS, qk),
                  ((0, 0), (0, 0), (0, LANES - qk))).reshape(MLA_Q_RANK, N_HEADS * LANES)
    ukv = w_ukv.reshape(MLA_KV_RANK, N_HEADS, MLA_NOPE + HEAD_V)
    wk2 = jnp.pad(ukv[:, :, :MLA_NOPE], ((0, 0), (0, 0), (0, LANES - MLA_NOPE))
                  ).reshape(MLA_KV_RANK, N_HEADS * LANES)
    wv2, bv2 = _vt_weight(ukv[:, :, MLA_NOPE:].reshape(MLA_KV_RANK, N_HEADS * HEAD_V), N_HEADS)
    return {"wmain": wmain.astype(BF16),
            "wt": jnp.concatenate([wt_b, wt_c, wt_d], axis=0).astype(BF16),
            "bt": jnp.concatenate([bt_b, bt_c, bt_d], axis=0),
            "qn": q_norm.reshape(1, -1), "kvn": kv_norm.reshape(1, -1),
            "wq2": wq2.astype(BF16), "wk2": wk2.astype(BF16), "wv2": wv2.astype(BF16), "bv2": bv2}


def _ffn_weights(w_gu, w_down):
    d, two_f = w_gu.shape
    d_ff = two_f // 2
    n_chunks = d_ff // FF_CHUNK
    gate = w_gu[:, :d_ff].reshape(d, n_chunks, FF_CHUNK)
    up = w_gu[:, d_ff:].reshape(d, n_chunks, FF_CHUNK)
    wgu_c = jnp.transpose(jnp.concatenate([gate, up], axis=2), (1, 0, 2)).astype(BF16)
    return wgu_c, w_down.astype(BF16)


def kernel(x, c, ctx, c_ctx, w_ada, b_ada, norm_ffn1, ffn1_w_gu, ffn1_w_down, norm_mix, w_in, mla_q_norm,
           mla_w_uq, mla_kv_norm, mla_w_ukv, diff_lam, diff_subln, na_rpb, gqa_sink, w_branch, w_gate,
           b_gate, w_out, norm_ffn2, ffn2_w_gu, ffn2_w_down, final_norm):
    n_batch, seq, d = x.shape
    ctx_len = ctx.shape[1]
    depth = w_ada.shape[0]
    rows = seq // GRID_W
    assert ctx_len == TK and seq % TM == 0 and (n_batch * ctx_len) % TM == 0
    assert rows % NA_Q_ROWS == 0 and rows >= NA_K_TILES * NA_Q_ROWS and seq // TQ >= 3
    n_lat = n_batch * seq
    dims = dict(seq=seq, n_batch=n_batch)

    r_pad = -(-(n_batch + 1) // 8) * 8
    cc = jnp.zeros((r_pad, d), F32).at[:n_batch].set(c).at[n_batch].set(c_ctx)
    mod = _modulation(cc, w_ada, b_ada).reshape(depth, r_pad, N_MOD, d)

    tabs = _rope_tables(seq, MLA_ROPE) + _rope_tables(seq, GQA_HEAD)
    h = jnp.concatenate([x.reshape(n_lat, d), ctx.reshape(n_batch * ctx_len, d)], axis=0)
    n_tok = h.shape[0]

    for l in range(depth):
        last = l == depth - 1
        lam_init = 0.8 - 0.6 * math.exp(-0.3 * l)
        mod_l = mod[l]
        row = lambda v: v.reshape(1, -1)
        wgu1, wdn1 = _ffn_weights(ffn1_w_gu[l], ffn1_w_down[l])
        wgu2, wdn2 = _ffn_weights(ffn2_w_gu[l], ffn2_w_down[l])

        h = _ffn(h, mod_l, row(norm_ffn1[l]), wgu1, wdn1, mod_base=0, n_tok_out=n_tok, **dims)

        mw = _mixin_weights(w_in[l], mla_q_norm[l], mla_w_uq[l], mla_kv_norm[l], mla_w_ukv[l])
        (qa, ka, vat, qb, kb, vbt, qc, kc, vct, qd, kd, vdt) = _mixin(
            h, mod_l, row(norm_mix[l]), tabs, mw, **dims)
        ya = _dense_attn(qa, ka, vat, diff=False, **dims)
        yb = _dense_attn(qb, kb, vbt, diff=True, lam_init=lam_init, lamp=diff_lam[l].astype(F32),
                         subln=diff_subln[l].reshape(HEAD_V, 1), **dims)
        yc = _na_attn(qc, kc, vct, _na_bias_tables(na_rpb[l], rows), **dims)
        yd = _gqa_attn(qd, kd, vdt, gqa_sink[l].astype(F32), **dims)
        h = _merge(h, mod_l, row(norm_mix[l]), (ya, yb, yc, yd), w_gate[l].astype(BF16),
                   b_gate[l].reshape(4, 1, d), w_branch[l].astype(BF16), w_out[l].astype(BF16), **dims)

        h = _ffn(h, mod_l, row(norm_ffn2[l]), wgu2, wdn2, mod_base=6,
                 n_tok_out=n_lat if last else n_tok, final_g=row(final_norm) if last else None, **dims)
    return h.reshape(n_batch, seq, d)
```

```python
import functools
import math

import numpy as np
import jax
import jax.numpy as jnp
from jax import lax
from jax.experimental import pallas as pl
from jax.experimental.pallas import tpu as pltpu

F32 = jnp.float32
BF16 = jnp.bfloat16

GRID_W = 64
N_MOD = 9
ROPE_BASE = 10000.0
NEG_INF = -1e30
LOG2E = math.log2(math.e)
EPS = 1e-6
SUBLN_EPS = 1e-5
N_HEADS = 4
HEAD_V = 64
MLA_Q_RANK = 256
MLA_KV_RANK = 128
MLA_NOPE = 64
MLA_ROPE = 32
MLA_SCALE = (MLA_NOPE + MLA_ROPE) ** -0.5
DIFF_HEAD = 32
DIFF_SCALE = DIFF_HEAD ** -0.5
NA_HEAD = 64
NA_ROWS = 8
NA_COLS = 16
NA_SCALE = NA_HEAD ** -0.5
GQA_KV_HEADS = 2
GQA_HEAD = 64
WINDOW = 128
GQA_SCALE = GQA_HEAD ** -0.5
BRANCH_W = 256
IN_SIZES = (256, 128, 32, 256, 256, 256, 256, 256, 256, 256, 128, 128)

LANES = 128
TQ = 256
TK = 256
DENSE_SUB = 2
VROWS = 80
ONES_ROW = HEAD_V
TM = 512
FF_CHUNK = 256
VMEM_LIMIT = 56 * 1024 * 1024


def _cparams(n_axes):
    return pltpu.CompilerParams(dimension_semantics=("arbitrary",) * n_axes,
                                vmem_limit_bytes=VMEM_LIMIT)


def _rms(x, eps):
    return x * lax.rsqrt(jnp.mean(x * x, axis=-1, keepdims=True) + eps)


def _modulate(h, g, shift, scale):
    return (_rms(h, EPS) * g) * (1.0 + scale) + shift


def _nt_dot(a, b):
    return lax.dot_general(a, b, (((1,), (1,)), ((), ())), preferred_element_type=F32)


def _mod_kernel(s_ref, w_ref, b_ref, o_ref):
    s = s_ref[...]
    s = s * jax.nn.sigmoid(s)
    o_ref[0] = jnp.dot(s, w_ref[0], preferred_element_type=F32,
                       precision=lax.Precision.HIGHEST) + b_ref[0]


def _modulation(cc, w_ada, b_ada):
    depth, d, nd = w_ada.shape
    r = cc.shape[0]
    return pl.pallas_call(
        _mod_kernel,
        grid=(depth, nd // d),
        in_specs=[pl.BlockSpec((r, d), lambda l, n: (0, 0)),
                  pl.BlockSpec((1, d, d), lambda l, n: (l, 0, n)),
                  pl.BlockSpec((1, 1, d), lambda l, n: (l, 0, n))],
        out_specs=pl.BlockSpec((1, r, d), lambda l, n: (l, 0, n)),
        out_shape=jax.ShapeDtypeStruct((depth, r, nd), F32),
        compiler_params=_cparams(2),
        name="adaln_mod",
    )(cc, w_ada, b_ada.reshape(depth, 1, nd))


def _ffn_kernel(*refs, mod_base, final):
    if final:
        h_ref, mod_ref, g_ref, wgu_ref, wdn_ref, fn_ref, o_ref, a_scr = refs
    else:
        h_ref, mod_ref, g_ref, wgu_ref, wdn_ref, o_ref, a_scr = refs
    h = h_ref[...]
    mod = mod_ref[0]
    shift = mod[mod_base:mod_base + 1]
    scale = mod[mod_base + 1:mod_base + 2]
    gate = mod[mod_base + 2:mod_base + 3]
    u = _modulate(h, g_ref[...], shift, scale).astype(BF16)
    for j in range(wgu_ref.shape[0]):
        gu = jnp.dot(u, wgu_ref[j], preferred_element_type=F32)
        g = gu[:, :FF_CHUNK]
        up = gu[:, FF_CHUNK:]
        a_scr[:, j * FF_CHUNK:(j + 1) * FF_CHUNK] = (g * jax.nn.sigmoid(g) * up).astype(BF16)
    y = jnp.dot(a_scr[...], wdn_ref[...], preferred_element_type=F32)
    out = h + (0.5 * gate) * y
    if final:
        out = _rms(out, EPS) * fn_ref[...]
    o_ref[...] = out


def _ffn(h, mod_l, g, wgu_c, wdn, *, mod_base, n_tok_out, seq, n_batch, final_g=None):
    d = h.shape[1]
    n_chunks = wgu_c.shape[0]
    d_ff = wdn.shape[0]
    final = final_g is not None
    const2 = lambda i: (0, 0)
    in_specs = [pl.BlockSpec((TM, d), lambda i: (i, 0)),
                pl.BlockSpec((1, N_MOD, d), lambda i: (jnp.minimum(i * TM // seq, n_batch), 0, 0)),
                pl.BlockSpec((1, d), const2),
                pl.BlockSpec((n_chunks, d, 2 * FF_CHUNK), lambda i: (0, 0, 0)),
                pl.BlockSpec((d_ff, d), const2)]
    args = [h, mod_l, g, wgu_c, wdn]
    if final:
        in_specs.append(pl.BlockSpec((1, d), const2))
        args.append(final_g)
    return pl.pallas_call(
        functools.partial(_ffn_kernel, mod_base=mod_base, final=final),
        grid=(n_tok_out // TM,),
        in_specs=in_specs,
        out_specs=pl.BlockSpec((TM, d), lambda i: (i, 0)),
        out_shape=jax.ShapeDtypeStruct((n_tok_out, d), F32),
        scratch_shapes=[pltpu.VMEM((TM, d_ff), BF16)],
        compiler_params=_cparams(1),
        name="ffn",
    )(*args)


def _rope(x, cos, sgn_sin, half):
    lane = lax.broadcasted_iota(jnp.int32, (1, LANES), 1)
    first = (lane & (2 * half - 1)) < half
    outs = []
    for g in range(x.shape[1] // LANES):
        xg = x[:, g * LANES:(g + 1) * LANES]
        partner = jnp.where(first, pltpu.roll(xg, LANES - half, 1), pltpu.roll(xg, half, 1))
        outs.append(xg * cos + partner * sgn_sin)
    return outs


def _mixin_kernel(h_ref, mod_ref, g_ref, c32_ref, s32_ref, c64_ref, s64_ref,
                  wmain_ref, wt_ref, bt_ref, qn_ref, kvn_ref, wq2_ref, wk2_ref, wv2_ref, bv2_ref,
                  qa_ref, ka_ref, vat_ref, qb_ref, kb_ref, vbt_ref, qc_ref, kc_ref, vct_ref,
                  qd_ref, kd_ref, vdt_ref):
    mod = mod_ref[0]
    u = _modulate(h_ref[...], g_ref[...], mod[3:4], mod[4:5]).astype(BF16)
    pm = jnp.dot(u, wmain_ref[...], preferred_element_type=F32)
    c32, s32, c64, s64 = c32_ref[...], s32_ref[...], c64_ref[...], s64_ref[...]
    lane = lax.broadcasted_iota(jnp.int32, (1, LANES), 1)
    rope_lanes = (lane >= MLA_NOPE) & (lane < MLA_NOPE + MLA_ROPE)
    c_mla = jnp.where(rope_lanes, c32, 1.0)
    s_mla = jnp.where(rope_lanes, s32, 0.0)

    cqn = (_rms(pm[:, 0:256], EPS) * qn_ref[...]).astype(BF16)
    ckvn = (_rms(pm[:, 256:384], EPS) * kvn_ref[...]).astype(BF16)
    qa = _rope(jnp.dot(cqn, wq2_ref[...], preferred_element_type=F32), c_mla, s_mla, MLA_ROPE // 4)
    k_rope = _rope(pm[:, 384:512], c_mla, s_mla, MLA_ROPE // 4)[0]
    ka = jnp.dot(ckvn, wk2_ref[...], preferred_element_type=F32)
    for hd in range(N_HEADS):
        sl = slice(hd * LANES, (hd + 1) * LANES)
        qa_ref[:, sl] = qa[hd].astype(BF16)
        ka_ref[:, sl] = (ka[:, sl] + k_rope).astype(BF16)
    vat = _nt_dot(wv2_ref[...], ckvn) + bv2_ref[...]

    for idx, (ref, lo, cos, sin, half) in enumerate((
            (qb_ref, 512, c32, s32, DIFF_HEAD // 4), (kb_ref, 768, c32, s32, DIFF_HEAD // 4),
            (qd_ref, 1536, c64, s64, GQA_HEAD // 4), (kd_ref, 1792, c64, s64, GQA_HEAD // 4))):
        parts = _rope(pm[:, lo:lo + 256], cos, sin, half)
        for g in range(2):
            ref[:, g * LANES:(g + 1) * LANES] = parts[g].astype(BF16)
    qc_ref[...] = pm[:, 1024:1280].astype(BF16)
    kc_ref[...] = pm[:, 1280:1536].astype(BF16)

    vt = _nt_dot(wt_ref[...], u) + bt_ref[...]
    nv = N_HEADS * VROWS
    for c in range(TM // TK):
        cs = slice(c * TK, (c + 1) * TK)
        vat_ref[c] = vat[:, cs].astype(BF16)
        vbt_ref[c] = vt[0:nv, cs].astype(BF16)
        vct_ref[c] = vt[nv:2 * nv, cs].astype(BF16)
        vdt_ref[c] = vt[2 * nv:, cs].astype(BF16)


def _mixin(h, mod_l, g, tabs, w, *, seq, n_batch):
    t, d = h.shape
    n_lat_tiles = n_batch * seq // TM
    tiles_per_seq = seq // TM
    tok = lambda i: (i, 0)
    const2 = lambda i: (0, 0)
    tab_map = lambda i: (jnp.where(i < n_lat_tiles, i % tiles_per_seq, tiles_per_seq), 0)
    vt_map = lambda i: (i, 0, 0)
    nv = N_HEADS * VROWS
    nvd = GQA_KV_HEADS * VROWS
    full = lambda a: pl.BlockSpec(a.shape, const2)
    in_specs = [pl.BlockSpec((TM, d), tok),
                pl.BlockSpec((1, N_MOD, d), lambda i: (jnp.minimum(i * TM // seq, n_batch), 0, 0)),
                pl.BlockSpec((1, d), const2)]
    in_specs += [pl.BlockSpec((TM, LANES), tab_map)] * 4
    wlist = [w["wmain"], w["wt"], w["bt"], w["qn"], w["kvn"], w["wq2"], w["wk2"], w["wv2"], w["bv2"]]
    in_specs += [full(a) for a in wlist]
    nat = lambda width: (pl.BlockSpec((TM, width), tok), jax.ShapeDtypeStruct((t, width), BF16))
    tr = lambda rows: (pl.BlockSpec((TM // TK, rows, TK), vt_map),
                       jax.ShapeDtypeStruct((t // TK, rows, TK), BF16))
    outs = [nat(512), nat(512), tr(nv), nat(256), nat(256), tr(nv), nat(256), nat(256), tr(nv),
            nat(256), nat(256), tr(nvd)]
    return pl.pallas_call(
        _mixin_kernel,
        grid=(t // TM,),
        in_specs=in_specs,
        out_specs=[o[0] for o in outs],
        out_shape=[o[1] for o in outs],
        compiler_params=_cparams(1),
        name="mixer_in",
    )(h, mod_l, g, *tabs, *wlist)


def _lane_masked(qg, lo, width):
    lane = lax.broadcasted_iota(jnp.int32, (1, LANES), 1)
    return jnp.where((lane >= lo) & (lane < lo + width), qg, jnp.zeros_like(qg))


def _softmax_tiles(scores, m_prev):
    m_new = m_prev
    for s in scores:
        mt = jnp.max(s, axis=0, keepdims=True)
        m_new = mt if m_new is None else jnp.maximum(m_new, mt)
    probs = [jnp.exp2(s - m_new).astype(BF16) for s in scores]
    alpha = None if m_prev is None else jnp.exp2(m_prev - m_new)
    return m_new, alpha, probs


def _pv(v_tiles, probs):
    out = None
    for vt, p in zip(v_tiles, probs):
        d = jnp.dot(vt, p, preferred_element_type=F32)
        out = d if out is None else out + d
    return out


def _normalised(acc, extra_denominator=None):
    den = acc[ONES_ROW:ONES_ROW + 1]
    if extra_denominator is not None:
        den = den + extra_denominator
    return acc[0:HEAD_V] / den


def _lanes(grp):
    return slice(grp * LANES, (grp + 1) * LANES)


def _vrows(hd):
    return slice(hd * VROWS, (hd + 1) * VROWS)


def _q_block_map(nq, n_batch):
    return lambda b, j: (jnp.where(j < nq, b * nq + j, n_batch * nq + b), 0)


def _dense_attn_kernel(*refs, diff, nq, lam_init):
    if diff:
        (lamp_ref, subln_ref, q_ref, kc_ref, kl_ref, vc_ref, vl_ref, o_ref,
         ot_scr, m_scr, acc_scr, sa_scr, sb_scr, qm_scr) = refs
        chains = [(hd // 2, hd) for hd in range(N_HEADS) for _ in range(2)]
        for c, (grp, hd) in enumerate(chains):
            lo = 2 * DIFF_HEAD * (hd % 2) + DIFF_HEAD * (c % 2)
            qm_scr[c] = _lane_masked(q_ref[:, _lanes(grp)], lo, DIFF_HEAD)
        query = lambda c: qm_scr[c]
    else:
        q_ref, kc_ref, kl_ref, vc_ref, vl_ref, o_ref, ot_scr, m_scr, acc_scr, sa_scr, sb_scr = refs
        chains = [(hd, hd) for hd in range(N_HEADS)]
        query = lambda c: q_ref[:, _lanes(c)]
    j = pl.program_id(1)
    n_steps = nq // DENSE_SUB
    n_pairs = jnp.where(j < nq, n_steps // 2, 0)
    n_chains = len(chains)

    def scores_of(k_tile, n_sub):
        k_cache = {}
        out = []
        for c, (grp, _) in enumerate(chains):
            row = []
            for i in range(n_sub):
                if (i, grp) not in k_cache:
                    k_cache[(i, grp)] = k_tile(i, grp)
                row.append(_nt_dot(k_cache[(i, grp)], query(c)))
            out.append(row)
        return out

    def accumulate(scores, v_tile, first):
        v_cache = {}
        soft = [_softmax_tiles(scores[c], None if first else m_scr[c]) for c in range(n_chains)]
        for c, (_, hd) in enumerate(chains):
            m_new, alpha, probs = soft[c]
            for i in range(len(probs)):
                if (i, hd) not in v_cache:
                    v_cache[(i, hd)] = v_tile(i, hd)
            pv = _pv([v_cache[(i, hd)] for i in range(len(probs))], probs)
            acc_scr[c] = pv if first else acc_scr[c] * alpha + pv
            m_scr[c] = m_new

    def score_step(step, dst):
        def k_tile(i, grp):
            off = pl.multiple_of((step * DENSE_SUB + i) * TK, TK)
            return kl_ref[pl.ds(off, TK), _lanes(grp)]

        sc = scores_of(k_tile, DENSE_SUB)
        for c in range(n_chains):
            for i in range(DENSE_SUB):
                dst[c, i] = sc[c][i]

    def consume_step(step, src):
        scores = [[src[c, i] for i in range(DENSE_SUB)] for c in range(n_chains)]
        accumulate(scores, lambda i, hd: vl_ref[step * DENSE_SUB + i, _vrows(hd), :], False)

    score_step(0, sa_scr)
    accumulate(scores_of(lambda i, grp: kc_ref[:, _lanes(grp)], 1),
               lambda i, hd: vc_ref[0, _vrows(hd), :], True)

    def body(t, carry):
        score_step(2 * t + 1, sb_scr)
        consume_step(2 * t, sa_scr)
        score_step(jnp.minimum(2 * t + 2, n_steps - 1), sa_scr)
        consume_step(2 * t + 1, sb_scr)
        return carry

    lax.fori_loop(0, n_pairs, body, 0)

    if diff:
        dl = lamp_ref[...]
        lam = (jnp.exp(jnp.sum(dl[0:1] * dl[1:2], axis=1, keepdims=True))
               - jnp.exp(jnp.sum(dl[2:3] * dl[3:4], axis=1, keepdims=True)) + lam_init)
    for hd in range(N_HEADS):
        if diff:
            o = _normalised(acc_scr[2 * hd]) - lam * _normalised(acc_scr[2 * hd + 1])
            o = o * lax.rsqrt(jnp.mean(o * o, axis=0, keepdims=True) + SUBLN_EPS)
            o = o * subln_ref[...] * (1.0 - lam_init)
        else:
            o = _normalised(acc_scr[hd])
        ot_scr[hd * HEAD_V:(hd + 1) * HEAD_V, :] = o
    o_ref[...] = ot_scr[...].T.astype(BF16)


def _dense_attn(q, k, vt, *, seq, n_batch, diff, lam_init=0.0, lamp=None, subln=None):
    t, qw = q.shape
    kw = k.shape[1]
    nq = seq // TQ
    n_lat_blocks = n_batch * nq
    nv = vt.shape[1]
    n_chains = 2 * N_HEADS if diff else N_HEADS
    assert nq % (2 * DENSE_SUB) == 0
    qmap = _q_block_map(nq, n_batch)
    in_specs = [pl.BlockSpec((TQ, qw), qmap),
                pl.BlockSpec((TK, kw), lambda b, j: (n_lat_blocks + b, 0)),
                pl.BlockSpec((seq, kw), lambda b, j: (b, 0)),
                pl.BlockSpec((1, nv, TK), lambda b, j: (n_lat_blocks + b, 0, 0)),
                pl.BlockSpec((nq, nv, TK), lambda b, j: (b, 0, 0))]
    args = [q, k, k, vt, vt]
    if diff:
        in_specs = [pl.BlockSpec(lamp.shape, lambda b, j: (0, 0)),
                    pl.BlockSpec(subln.shape, lambda b, j: (0, 0))] + in_specs
        args = [lamp, subln] + args
    return pl.pallas_call(
        functools.partial(_dense_attn_kernel, diff=diff, nq=nq, lam_init=lam_init),
        grid=(n_batch, nq + 1),
        in_specs=in_specs,
        out_specs=pl.BlockSpec((TQ, BRANCH_W), qmap),
        out_shape=jax.ShapeDtypeStruct((t, BRANCH_W), BF16),
        scratch_shapes=([pltpu.VMEM((BRANCH_W, TQ), F32),
                         pltpu.VMEM((n_chains, 1, TQ), F32),
                         pltpu.VMEM((n_chains, VROWS, TQ), F32),
                         pltpu.VMEM((n_chains, DENSE_SUB, TK, TQ), F32),
                         pltpu.VMEM((n_chains, DENSE_SUB, TK, TQ), F32)]
                        + ([pltpu.VMEM((n_chains, TQ, LANES), BF16)] if diff else [])),
        compiler_params=_cparams(2),
        name="diff_attn" if diff else "mla_attn",
    )(*args)


NA_Q_ROWS = TQ // GRID_W
NA_K_TILES = 3


def _na_attn_kernel(q_ref, kc_ref, k0_ref, k1_ref, k2_ref, vc_ref, v0_ref, v1_ref, v2_ref, bias_ref,
                    o_ref, ot_scr):
    k_refs = (kc_ref, k0_ref, k1_ref, k2_ref)
    v_refs = (vc_ref, v0_ref, v1_ref, v2_ref)
    keys = [[r[:, _lanes(grp)] for r in k_refs] for grp in range(N_HEADS // 2)]
    scores = []
    for hd in range(N_HEADS):
        grp = hd // 2
        qm = _lane_masked(q_ref[:, _lanes(grp)], NA_HEAD * (hd % 2), NA_HEAD)
        sc = [_nt_dot(kt, qm) for kt in keys[grp]]
        scores.append([sc[0]] + [sc[1 + t] + bias_ref[0, hd, t * TK:(t + 1) * TK, :]
                                 for t in range(NA_K_TILES)])
    soft = [_softmax_tiles(sc, None) for sc in scores]
    for hd in range(N_HEADS):
        acc = _pv([r[0, _vrows(hd), :] for r in v_refs], soft[hd][2])
        ot_scr[hd * HEAD_V:(hd + 1) * HEAD_V, :] = _normalised(acc)
    o_ref[...] = ot_scr[...].T.astype(BF16)


def _na_bias_tables(rpb, rows):
    n_heads = rpb.shape[0]
    exact = lax.Precision.HIGHEST
    kc = np.arange(GRID_W)[:, None]
    c = np.arange(GRID_W)[None, :]
    cs = np.clip(c - NA_COLS // 2, 0, GRID_W - NA_COLS)
    col_ok = (kc >= cs) & (kc < cs + NA_COLS)
    dc = np.clip(kc - c + NA_COLS - 1, 0, 2 * NA_COLS - 2)
    pick_dc = (dc[..., None] == np.arange(2 * NA_COLS - 1)).astype(np.float32)
    toe = jnp.einsum("hde,xce->hdxc", rpb.astype(F32), pick_dc, precision=exact)
    tabs = []
    for r0 in (0, NA_Q_ROWS, rows - NA_Q_ROWS):
        base = int(np.clip(r0 - NA_ROWS // 2, 0, rows - NA_K_TILES * NA_Q_ROWS))
        kr = (base + np.arange(NA_K_TILES * NA_Q_ROWS))[:, None]
        r = (r0 + np.arange(NA_Q_ROWS))[None, :]
        rs = np.clip(r - NA_ROWS // 2, 0, rows - NA_ROWS)
        row_ok = (kr >= rs) & (kr < rs + NA_ROWS)
        dr = np.clip(kr - r + NA_ROWS - 1, 0, 2 * NA_ROWS - 2)
        pick_dr = (dr[..., None] == np.arange(2 * NA_ROWS - 1)).astype(np.float32)
        bias = jnp.einsum("krd,hdxc->hkxrc", pick_dr, toe, precision=exact)
        ok = row_ok[:, None, :, None] & col_ok[None, :, None, :]
        tabs.append(jnp.where(jnp.asarray(ok)[None], bias * LOG2E, NEG_INF).reshape(n_heads, NA_K_TILES * TK, TQ))
    tabs.append(jnp.full_like(tabs[0], NEG_INF))
    return jnp.stack(tabs)


def _na_attn(q, k, vt, bias, *, seq, n_batch):
    t, w = q.shape
    nq = seq // TQ
    n_lat_blocks = n_batch * nq
    nv = vt.shape[1]
    qmap = _q_block_map(nq, n_batch)

    def kblock(tile):
        return lambda b, j: (b * nq + jnp.clip(j - 1, 0, nq - NA_K_TILES) + tile, 0)

    def vblock(tile):
        return lambda b, j: (b * nq + jnp.clip(j - 1, 0, nq - NA_K_TILES) + tile, 0, 0)

    variant = lambda b, j: (jnp.where(j == 0, 0, jnp.where(j == nq - 1, 2, jnp.where(j == nq, 3, 1))),
                            0, 0, 0)
    in_specs = ([pl.BlockSpec((TQ, w), qmap),
                 pl.BlockSpec((TK, w), lambda b, j: (n_lat_blocks + b, 0))]
                + [pl.BlockSpec((TK, w), kblock(i)) for i in range(NA_K_TILES)]
                + [pl.BlockSpec((1, nv, TK), lambda b, j: (n_lat_blocks + b, 0, 0))]
                + [pl.BlockSpec((1, nv, TK), vblock(i)) for i in range(NA_K_TILES)]
                + [pl.BlockSpec((1,) + bias.shape[1:], variant)])
    return pl.pallas_call(
        _na_attn_kernel,
        grid=(n_batch, nq + 1),
        in_specs=in_specs,
        out_specs=pl.BlockSpec((TQ, BRANCH_W), qmap),
        out_shape=jax.ShapeDtypeStruct((t, BRANCH_W), BF16),
        scratch_shapes=[pltpu.VMEM((BRANCH_W, TQ), F32)],
        compiler_params=_cparams(2),
        name="na_attn",
    )(q, k, k, k, k, vt, vt, vt, vt, bias)


def _gqa_attn_kernel(sink_ref, q_ref, kc_ref, k0_ref, k1_ref, k2_ref, vc_ref, v0_ref, v1_ref, v2_ref,
                     o_ref, ot_scr, *, nq):
    j = pl.program_id(1)
    k_refs = (kc_ref, k0_ref, k1_ref, k2_ref)
    v_refs = (vc_ref, v0_ref, v1_ref, v2_ref)
    key_i = lax.broadcasted_iota(jnp.int32, (TK, TQ), 0)
    qry_i = lax.broadcasted_iota(jnp.int32, (TK, TQ), 1)
    keeps = []
    for rel in (-1, 0, 1):
        tile = j + rel
        tile_ok = (tile >= 0) & (tile < nq) & (j < nq)
        reach = jnp.where(tile_ok, WINDOW, -1)
        keeps.append(jnp.abs(rel * TK + key_i - qry_i) <= reach)
    heads_per_kv = N_HEADS // GQA_KV_HEADS
    keys = [[r[:, _lanes(kv)] for r in k_refs] for kv in range(GQA_KV_HEADS)]
    scores = []
    for hd in range(N_HEADS):
        kv = hd // heads_per_kv
        qm = _lane_masked(q_ref[:, _lanes(kv)], GQA_HEAD * (hd % heads_per_kv), GQA_HEAD)
        sc = [_nt_dot(kt, qm) for kt in keys[kv]]
        scores.append([sc[0]] + [jnp.where(keeps[t], sc[1 + t], NEG_INF) for t in range(3)])
    soft = [_softmax_tiles(scores[hd], jnp.full((1, TQ), sink_ref[hd] * LOG2E, F32))
            for hd in range(N_HEADS)]
    for hd in range(N_HEADS):
        _, sink_weight, probs = soft[hd]
        acc = _pv([r[0, _vrows(hd // heads_per_kv), :] for r in v_refs], probs)
        ot_scr[hd * HEAD_V:(hd + 1) * HEAD_V, :] = _normalised(acc, sink_weight)
    o_ref[...] = ot_scr[...].T.astype(BF16)


def _gqa_attn(q, k, vt, sink, *, seq, n_batch):
    t, w = q.shape
    nq = seq // TQ
    n_lat_blocks = n_batch * nq
    nv = vt.shape[1]
    qmap = _q_block_map(nq, n_batch)

    def kblock(rel):
        return lambda b, j: (b * nq + jnp.clip(j + rel, 0, nq - 1), 0)

    def vblock(rel):
        return lambda b, j: (b * nq + jnp.clip(j + rel, 0, nq - 1), 0, 0)

    in_specs = ([pl.BlockSpec(memory_space=pltpu.SMEM),
                 pl.BlockSpec((TQ, w), qmap),
                 pl.BlockSpec((TK, w), lambda b, j: (n_lat_blocks + b, 0))]
                + [pl.BlockSpec((TK, w), kblock(rel)) for rel in (-1, 0, 1)]
                + [pl.BlockSpec((1, nv, TK), lambda b, j: (n_lat_blocks + b, 0, 0))]
                + [pl.BlockSpec((1, nv, TK), vblock(rel)) for rel in (-1, 0, 1)])
    return pl.pallas_call(
        functools.partial(_gqa_attn_kernel, nq=nq),
        grid=(n_batch, nq + 1),
        in_specs=in_specs,
        out_specs=pl.BlockSpec((TQ, BRANCH_W), qmap),
        out_shape=jax.ShapeDtypeStruct((t, BRANCH_W), BF16),
        scratch_shapes=[pltpu.VMEM((BRANCH_W, TQ), F32)],
        compiler_params=_cparams(2),
        name="gqa_attn",
    )(sink, q, k, k, k, k, vt, vt, vt, vt)


def _merge_kernel(h_ref, mod_ref, g_ref, ya_ref, yb_ref, yc_ref, yd_ref, wg_ref, bg_ref, wb_ref, wo_ref,
                  o_ref):
    h = h_ref[...]
    mod = mod_ref[0]
    u = _modulate(h, g_ref[...], mod[3:4], mod[4:5]).astype(BF16)
    merged = None
    for i, y_ref in enumerate((ya_ref, yb_ref, yc_ref, yd_ref)):
        gate = jax.nn.sigmoid(jnp.dot(u, wg_ref[i], preferred_element_type=F32) + bg_ref[i])
        term = gate * jnp.dot(y_ref[...], wb_ref[i], preferred_element_type=F32)
        merged = term if merged is None else merged + term
    y = jnp.dot(merged.astype(BF16), wo_ref[...], preferred_element_type=F32)
    o_ref[...] = h + mod[5:6] * y


def _merge(h, mod_l, g, ys, wg, bg, wb, wo, *, seq, n_batch):
    t, d = h.shape
    tok = lambda i: (i, 0)
    const2 = lambda i: (0, 0)
    const3 = lambda i: (0, 0, 0)
    in_specs = ([pl.BlockSpec((TM, d), tok),
                 pl.BlockSpec((1, N_MOD, d), lambda i: (jnp.minimum(i * TM // seq, n_batch), 0, 0)),
                 pl.BlockSpec((1, d), const2)]
                + [pl.BlockSpec((TM, BRANCH_W), tok)] * 4
                + [pl.BlockSpec(wg.shape, const3), pl.BlockSpec(bg.shape, const3),
                   pl.BlockSpec(wb.shape, const3), pl.BlockSpec(wo.shape, const2)])
    return pl.pallas_call(
        _merge_kernel,
        grid=(t // TM,),
        in_specs=in_specs,
        out_specs=pl.BlockSpec((TM, d), tok),
        out_shape=jax.ShapeDtypeStruct((t, d), F32),
        compiler_params=_cparams(1),
        name="merge",
    )(h, mod_l, g, *ys, wg, bg, wb, wo)


def _rope_tables(seq, dim):
    tpos = jnp.arange(seq)
    rows = (tpos // GRID_W).astype(F32)
    cols = (tpos % GRID_W).astype(F32)
    half = dim // 2
    freqs = jnp.power(ROPE_BASE, -jnp.arange(0, half, 2, dtype=F32) / half)
    ang = jnp.concatenate([rows[:, None] * freqs] * 2 + [cols[:, None] * freqs] * 2, axis=1)
    nfreq = dim // 4
    sign = np.where((np.arange(dim) % (2 * nfreq)) < nfreq, -1.0, 1.0).astype(np.float32)
    reps = LANES // dim
    cos = jnp.tile(jnp.cos(ang), (1, reps))
    sin = jnp.tile(jnp.sin(ang) * sign, (1, reps))
    cos = jnp.concatenate([cos, jnp.ones((TM, LANES), F32)], axis=0)
    sin = jnp.concatenate([sin, jnp.zeros((TM, LANES), F32)], axis=0)
    return cos, sin


def _vt_weight(w_cols, n_heads):
    d_in = w_cols.shape[0]
    wt = w_cols.T.reshape(n_heads, HEAD_V, d_in)
    wt = jnp.pad(wt, ((0, 0), (0, VROWS - HEAD_V), (0, 0))).reshape(n_heads * VROWS, d_in)
    bias = np.zeros((n_heads, VROWS, 1), np.float32)
    bias[:, ONES_ROW] = 1.0
    return wt, jnp.asarray(bias.reshape(n_heads * VROWS, 1))


def _mixin_weights(w_in, q_norm, w_uq, kv_norm, w_ukv):
    offs = np.concatenate([[0], np.cumsum(IN_SIZES)])
    col = lambda i: w_in[:, offs[i]:offs[i + 1]]
    d = w_in.shape[0]
    kr = jnp.zeros((d, LANES), F32).at[:, MLA_NOPE:MLA_NOPE + MLA_ROPE].set(col(2))
    gk = col(10).reshape(d, GQA_KV_HEADS, 1, GQA_HEAD)
    gk2 = jnp.broadcast_to(gk, (d, GQA_KV_HEADS, 2, GQA_HEAD)).reshape(d, 2 * GQA_KV_HEADS * GQA_HEAD)
    wmain = jnp.concatenate([col(0), col(1), kr, col(3) * (DIFF_SCALE * LOG2E), col(4),
                             col(6) * (NA_SCALE * LOG2E), col(7), col(9) * (GQA_SCALE * LOG2E), gk2], axis=1)
    wt_b, bt_b = _vt_weight(col(5), N_HEADS)
    wt_c, bt_c = _vt_weight(col(8), N_HEADS)
    wt_d, bt_d = _vt_weight(col(11), GQA_KV_HEADS)
    qk = MLA_NOPE + MLA_ROPE
    wq2 = jnp.pad((w_uq * (MLA_SCALE * LOG2E)).reshape(MLA_Q_RANK, N_HEADS, qk),
                  ((0, 0), (0, 0), (0, LANES - qk))).reshape(MLA_Q_RANK, N_HEADS * LANES)
    ukv = w_ukv.reshape(MLA_KV_RANK, N_HEADS, MLA_NOPE + HEAD_V)
    wk2 = jnp.pad(ukv[:, :, :MLA_NOPE], ((0, 0), (0, 0), (0, LANES - MLA_NOPE))
                  ).reshape(MLA_KV_RANK, N_HEADS * LANES)
    wv2, bv2 = _vt_weight(ukv[:, :, MLA_NOPE:].reshape(MLA_KV_RANK, N_HEADS * HEAD_V), N_HEADS)
    return {"wmain": wmain.astype(BF16),
            "wt": jnp.concatenate([wt_b, wt_c, wt_d], axis=0).astype(BF16),
            "bt": jnp.concatenate([bt_b, bt_c, bt_d], axis=0),
            "qn": q_norm.reshape(1, -1), "kvn": kv_norm.reshape(1, -1),
            "wq2": wq2.astype(BF16), "wk2": wk2.astype(BF16), "wv2": wv2.astype(BF16), "bv2": bv2}


def _ffn_weights(w_gu, w_down):
    d, two_f = w_gu.shape
    d_ff = two_f // 2
    n_chunks = d_ff // FF_CHUNK
    gate = w_gu[:, :d_ff].reshape(d, n_chunks, FF_CHUNK)
    up = w_gu[:, d_ff:].reshape(d, n_chunks, FF_CHUNK)
    wgu_c = jnp.transpose(jnp.concatenate([gate, up], axis=2), (1, 0, 2)).astype(BF16)
    return wgu_c, w_down.astype(BF16)


def kernel(x, c, ctx, c_ctx, w_ada, b_ada, norm_ffn1, ffn1_w_gu, ffn1_w_down, norm_mix, w_in, mla_q_norm,
           mla_w_uq, mla_kv_norm, mla_w_ukv, diff_lam, diff_subln, na_rpb, gqa_sink, w_branch, w_gate,
           b_gate, w_out, norm_ffn2, ffn2_w_gu, ffn2_w_down, final_norm):
    n_batch, seq, d = x.shape
    ctx_len = ctx.shape[1]
    depth = w_ada.shape[0]
    rows = seq // GRID_W
    assert ctx_len == TK and seq % TM == 0 and (n_batch * ctx_len) % TM == 0
    assert rows % NA_Q_ROWS == 0 and rows >= NA_K_TILES * NA_Q_ROWS and seq // TQ >= 3
    n_lat = n_batch * seq
    dims = dict(seq=seq, n_batch=n_batch)

    r_pad = -(-(n_batch + 1) // 8) * 8
    cc = jnp.zeros((r_pad, d), F32).at[:n_batch].set(c).at[n_batch].set(c_ctx)
    mod = _modulation(cc, w_ada, b_ada).reshape(depth, r_pad, N_MOD, d)

    tabs = _rope_tables(seq, MLA_ROPE) + _rope_tables(seq, GQA_HEAD)
    h = jnp.concatenate([x.reshape(n_lat, d), ctx.reshape(n_batch * ctx_len, d)], axis=0)
    n_tok = h.shape[0]

    for l in range(depth):
        last = l == depth - 1
        lam_init = 0.8 - 0.6 * math.exp(-0.3 * l)
        mod_l = mod[l]
        row = lambda v: v.reshape(1, -1)
        wgu1, wdn1 = _ffn_weights(ffn1_w_gu[l], ffn1_w_down[l])
        wgu2, wdn2 = _ffn_weights(ffn2_w_gu[l], ffn2_w_down[l])

        h = _ffn(h, mod_l, row(norm_ffn1[l]), wgu1, wdn1, mod_base=0, n_tok_out=n_tok, **dims)

        mw = _mixin_weights(w_in[l], mla_q_norm[l], mla_w_uq[l], mla_kv_norm[l], mla_w_ukv[l])
        (qa, ka, vat, qb, kb, vbt, qc, kc, vct, qd, kd, vdt) = _mixin(
            h, mod_l, row(norm_mix[l]), tabs, mw, **dims)
        ya = _dense_attn(qa, ka, vat, diff=False, **dims)
        yb = _dense_attn(qb, kb, vbt, diff=True, lam_init=lam_init, lamp=diff_lam[l].astype(F32),
                         subln=diff_subln[l].reshape(HEAD_V, 1), **dims)
        yc = _na_attn(qc, kc, vct, _na_bias_tables(na_rpb[l], rows), **dims)
        yd = _gqa_attn(qd, kd, vdt, gqa_sink[l].astype(F32), **dims)
        h = _merge(h, mod_l, row(norm_mix[l]), (ya, yb, yc, yd), w_gate[l].astype(BF16),
                   b_gate[l].reshape(4, 1, d), w_branch[l].astype(BF16), w_out[l].astype(BF16), **dims)

        h = _ffn(h, mod_l, row(norm_ffn2[l]), wgu2, wdn2, mod_base=6,
                 n_tok_out=n_lat if last else n_tok, final_g=row(final_norm) if last else None, **dims)
    return h.reshape(n_batch, seq, d)
```

```python
import functools
import math

import numpy as np
import jax
import jax.numpy as jnp
from jax import lax
from jax.experimental import pallas as pl
from jax.experimental.pallas import tpu as pltpu

F32 = jnp.float32
BF16 = jnp.bfloat16

GRID_W = 64
N_MOD = 9
ROPE_BASE = 10000.0
NEG_INF = -1e30
LOG2E = math.log2(math.e)
EPS = 1e-6
SUBLN_EPS = 1e-5
N_HEADS = 4
HEAD_V = 64
MLA_Q_RANK = 256
MLA_KV_RANK = 128
MLA_NOPE = 64
MLA_ROPE = 32
MLA_SCALE = (MLA_NOPE + MLA_ROPE) ** -0.5
DIFF_HEAD = 32
DIFF_SCALE = DIFF_HEAD ** -0.5
NA_HEAD = 64
NA_ROWS = 8
NA_COLS = 16
NA_SCALE = NA_HEAD ** -0.5
GQA_KV_HEADS = 2
GQA_HEAD = 64
WINDOW = 128
GQA_SCALE = GQA_HEAD ** -0.5
BRANCH_W = 256
IN_SIZES = (256, 128, 32, 256, 256, 256, 256, 256, 256, 256, 128, 128)

LANES = 128
TQ = 256
TK = 256
DENSE_SUB = 2
VROWS = 80
ONES_ROW = HEAD_V
TM = 512
FF_CHUNK = 256
VMEM_LIMIT = 56 * 1024 * 1024


def _cparams(n_axes):
    return pltpu.CompilerParams(dimension_semantics=("arbitrary",) * n_axes,
                                vmem_limit_bytes=VMEM_LIMIT)


def _rms(x, eps):
    return x * lax.rsqrt(jnp.mean(x * x, axis=-1, keepdims=True) + eps)


def _modulate(h, g, shift, scale):
    return (_rms(h, EPS) * g) * (1.0 + scale) + shift


def _nt_dot(a, b):
    return lax.dot_general(a, b, (((1,), (1,)), ((), ())), preferred_element_type=F32)


def _mod_kernel(s_ref, w_ref, b_ref, o_ref):
    s = s_ref[...]
    s = s * jax.nn.sigmoid(s)
    o_ref[0] = jnp.dot(s, w_ref[0], preferred_element_type=F32,
                       precision=lax.Precision.HIGHEST) + b_ref[0]


def _modulation(cc, w_ada, b_ada):
    depth, d, nd = w_ada.shape
    r = cc.shape[0]
    return pl.pallas_call(
        _mod_kernel,
        grid=(depth, nd // d),
        in_specs=[pl.BlockSpec((r, d), lambda l, n: (0, 0)),
                  pl.BlockSpec((1, d, d), lambda l, n: (l, 0, n)),
                  pl.BlockSpec((1, 1, d), lambda l, n: (l, 0, n))],
        out_specs=pl.BlockSpec((1, r, d), lambda l, n: (l, 0, n)),
        out_shape=jax.ShapeDtypeStruct((depth, r, nd), F32),
        compiler_params=_cparams(2),
        name="adaln_mod",
    )(cc, w_ada, b_ada.reshape(depth, 1, nd))


def _ffn_kernel(*refs, mod_base, final):
    if final:
        h_ref, mod_ref, g_ref, wgu_ref, wdn_ref, fn_ref, o_ref, a_scr = refs
    else:
        h_ref, mod_ref, g_ref, wgu_ref, wdn_ref, o_ref, a_scr = refs
    h = h_ref[...]
    mod = mod_ref[0]
    shift = mod[mod_base:mod_base + 1]
    scale = mod[mod_base + 1:mod_base + 2]
    gate = mod[mod_base + 2:mod_base + 3]
    u = _modulate(h, g_ref[...], shift, scale).astype(BF16)
    for j in range(wgu_ref.shape[0]):
        gu = jnp.dot(u, wgu_ref[j], preferred_element_type=F32)
        g = gu[:, :FF_CHUNK]
        up = gu[:, FF_CHUNK:]
        a_scr[:, j * FF_CHUNK:(j + 1) * FF_CHUNK] = (g * jax.nn.sigmoid(g) * up).astype(BF16)
    y = jnp.dot(a_scr[...], wdn_ref[...], preferred_element_type=F32)
    out = h + (0.5 * gate) * y
    if final:
        out = _rms(out, EPS) * fn_ref[...]
    o_ref[...] = out


def _ffn(h, mod_l, g, wgu_c, wdn, *, mod_base, n_tok_out, seq, n_batch, final_g=None):
    d = h.shape[1]
    n_chunks = wgu_c.shape[0]
    d_ff = wdn.shape[0]
    final = final_g is not None
    const2 = lambda i: (0, 0)
    in_specs = [pl.BlockSpec((TM, d), lambda i: (i, 0)),
                pl.BlockSpec((1, N_MOD, d), lambda i: (jnp.minimum(i * TM // seq, n_batch), 0, 0)),
                pl.BlockSpec((1, d), const2),
                pl.BlockSpec((n_chunks, d, 2 * FF_CHUNK), lambda i: (0, 0, 0)),
                pl.BlockSpec((d_ff, d), const2)]
    args = [h, mod_l, g, wgu_c, wdn]
    if final:
        in_specs.append(pl.BlockSpec((1, d), const2))
        args.append(final_g)
    return pl.pallas_call(
        functools.partial(_ffn_kernel, mod_base=mod_base, final=final),
        grid=(n_tok_out // TM,),
        in_specs=in_specs,
        out_specs=pl.BlockSpec((TM, d), lambda i: (i, 0)),
        out_shape=jax.ShapeDtypeStruct((n_tok_out, d), F32),
        scratch_shapes=[pltpu.VMEM((TM, d_ff), BF16)],
        compiler_params=_cparams(1),
        name="ffn",
    )(*args)


def _rope(x, cos, sgn_sin, half):
    lane = lax.broadcasted_iota(jnp.int32, (1, LANES), 1)
    first = (lane & (2 * half - 1)) < half
    outs = []
    for g in range(x.shape[1] // LANES):
        xg = x[:, g * LANES:(g + 1) * LANES]
        partner = jnp.where(first, pltpu.roll(xg, LANES - half, 1), pltpu.roll(xg, half, 1))
        outs.append(xg * cos + partner * sgn_sin)
    return outs


def _mixin_kernel(h_ref, mod_ref, g_ref, c32_ref, s32_ref, c64_ref, s64_ref,
                  wmain_ref, wt_ref, bt_ref, qn_ref, kvn_ref, wq2_ref, wk2_ref, wv2_ref, bv2_ref,
                  qa_ref, ka_ref, vat_ref, qb_ref, kb_ref, vbt_ref, qc_ref, kc_ref, vct_ref,
                  qd_ref, kd_ref, vdt_ref):
    mod = mod_ref[0]
    u = _modulate(h_ref[...], g_ref[...], mod[3:4], mod[4:5]).astype(BF16)
    pm = jnp.dot(u, wmain_ref[...], preferred_element_type=F32)
    c32, s32, c64, s64 = c32_ref[...], s32_ref[...], c64_ref[...], s64_ref[...]
    lane = lax.broadcasted_iota(jnp.int32, (1, LANES), 1)
    rope_lanes = (lane >= MLA_NOPE) & (lane < MLA_NOPE + MLA_ROPE)
    c_mla = jnp.where(rope_lanes, c32, 1.0)
    s_mla = jnp.where(rope_lanes, s32, 0.0)

    cqn = (_rms(pm[:, 0:256], EPS) * qn_ref[...]).astype(BF16)
    ckvn = (_rms(pm[:, 256:384], EPS) * kvn_ref[...]).astype(BF16)
    qa = _rope(jnp.dot(cqn, wq2_ref[...], preferred_element_type=F32), c_mla, s_mla, MLA_ROPE // 4)
    k_rope = _rope(pm[:, 384:512], c_mla, s_mla, MLA_ROPE // 4)[0]
    ka = jnp.dot(ckvn, wk2_ref[...], preferred_element_type=F32)
    for hd in range(N_HEADS):
        sl = slice(hd * LANES, (hd + 1) * LANES)
        qa_ref[:, sl] = qa[hd].astype(BF16)
        ka_ref[:, sl] = (ka[:, sl] + k_rope).astype(BF16)
    vat = _nt_dot(wv2_ref[...], ckvn) + bv2_ref[...]

    for idx, (ref, lo, cos, sin, half) in enumerate((
            (qb_ref, 512, c32, s32, DIFF_HEAD // 4), (kb_ref, 768, c32, s32, DIFF_HEAD // 4),
            (qd_ref, 1536, c64, s64, GQA_HEAD // 4), (kd_ref, 1792, c64, s64, GQA_HEAD // 4))):
        parts = _rope(pm[:, lo:lo + 256], cos, sin, half)
        for g in range(2):
            ref[:, g * LANES:(g + 1) * LANES] = parts[g].astype(BF16)
    qc_ref[...] = pm[:, 1024:1280].astype(BF16)
    kc_ref[...] = pm[:, 1280:1536].astype(BF16)

    vt = _nt_dot(wt_ref[...], u) + bt_ref[...]
    nv = N_HEADS * VROWS
    for c in range(TM // TK):
        cs = slice(c * TK, (c + 1) * TK)
        vat_ref[c] = vat[:, cs].astype(BF16)
        vbt_ref[c] = vt[0:nv, cs].astype(BF16)
        vct_ref[c] = vt[nv:2 * nv, cs].astype(BF16)
        vdt_ref[c] = vt[2 * nv:, cs].astype(BF16)


def _mixin(h, mod_l, g, tabs, w, *, seq, n_batch):
    t, d = h.shape
    n_lat_tiles = n_batch * seq // TM
    tiles_per_seq = seq // TM
    tok = lambda i: (i, 0)
    const2 = lambda i: (0, 0)
    tab_map = lambda i: (jnp.where(i < n_lat_tiles, i % tiles_per_seq, tiles_per_seq), 0)
    vt_map = lambda i: (i, 0, 0)
    nv = N_HEADS * VROWS
    nvd = GQA_KV_HEADS * VROWS
    full = lambda a: pl.BlockSpec(a.shape, const2)
    in_specs = [pl.BlockSpec((TM, d), tok),
                pl.BlockSpec((1, N_MOD, d), lambda i: (jnp.minimum(i * TM // seq, n_batch), 0, 0)),
                pl.BlockSpec((1, d), const2)]
    in_specs += [pl.BlockSpec((TM, LANES), tab_map)] * 4
    wlist = [w["wmain"], w["wt"], w["bt"], w["qn"], w["kvn"], w["wq2"], w["wk2"], w["wv2"], w["bv2"]]
    in_specs += [full(a) for a in wlist]
    nat = lambda width: (pl.BlockSpec((TM, width), tok), jax.ShapeDtypeStruct((t, width), BF16))
    tr = lambda rows: (pl.BlockSpec((TM // TK, rows, TK), vt_map),
                       jax.ShapeDtypeStruct((t // TK, rows, TK), BF16))
    outs = [nat(512), nat(512), tr(nv), nat(256), nat(256), tr(nv), nat(256), nat(256), tr(nv),
            nat(256), nat(256), tr(nvd)]
    return pl.pallas_call(
        _mixin_kernel,
        grid=(t // TM,),
        in_specs=in_specs,
        out_specs=[o[0] for o in outs],
        out_shape=[o[1] for o in outs],
        compiler_params=_cparams(1),
        name="mixer_in",
    )(h, mod_l, g, *tabs, *wlist)


def _lane_masked(qg, lo, width):
    lane = lax.broadcasted_iota(jnp.int32, (1, LANES), 1)
    return jnp.where((lane >= lo) & (lane < lo + width), qg, jnp.zeros_like(qg))


def _interleave(n_chains, produce, consume, lag):
    for idx in range(n_chains + lag):
        if idx < n_chains:
            produce(idx)
        if idx >= lag:
            consume(idx - lag)


def _softmax_tiles(scores, m_prev):
    m_new = m_prev
    for s in scores:
        mt = jnp.max(s, axis=0, keepdims=True)
        m_new = mt if m_new is None else jnp.maximum(m_new, mt)
    probs = [jnp.exp2(s - m_new).astype(BF16) for s in scores]
    alpha = None if m_prev is None else jnp.exp2(m_prev - m_new)
    return m_new, alpha, probs


def _pv(v_tiles, probs):
    out = None
    for vt, p in zip(v_tiles, probs):
        d = jnp.dot(vt, p, preferred_element_type=F32)
        out = d if out is None else out + d
    return out


def _normalised(acc, extra_denominator=None):
    den = acc[ONES_ROW:ONES_ROW + 1]
    if extra_denominator is not None:
        den = den + extra_denominator
    return acc[0:HEAD_V] / den


def _lanes(grp):
    return slice(grp * LANES, (grp + 1) * LANES)


def _vrows(hd):
    return slice(hd * VROWS, (hd + 1) * VROWS)


def _q_block_map(nq, n_batch):
    return lambda b, j: (jnp.where(j < nq, b * nq + j, n_batch * nq + b), 0)


def _dense_attn_kernel(*refs, diff, nq, lam_init):
    if diff:
        (lamp_ref, subln_ref, q_ref, kc_ref, kl_ref, vc_ref, vl_ref, o_ref,
         ot_scr, m_scr, acc_scr, sa_scr, sb_scr, qm_scr) = refs
        chains = [(hd // 2, hd) for hd in range(N_HEADS) for _ in range(2)]
        for c, (grp, hd) in enumerate(chains):
            lo = 2 * DIFF_HEAD * (hd % 2) + DIFF_HEAD * (c % 2)
            qm_scr[c] = _lane_masked(q_ref[:, _lanes(grp)], lo, DIFF_HEAD)
        query = lambda c: qm_scr[c]
    else:
        q_ref, kc_ref, kl_ref, vc_ref, vl_ref, o_ref, ot_scr, m_scr, acc_scr, sa_scr, sb_scr = refs
        chains = [(hd, hd) for hd in range(N_HEADS)]
        query = lambda c: q_ref[:, _lanes(c)]
    j = pl.program_id(1)
    n_steps = nq // DENSE_SUB
    n_live = jnp.where(j < nq, n_steps, 0)
    n_chains = len(chains)

    def score_into(dst, step):
        k_cache = {}

        def produce(c):
            grp = chains[c][0]
            for i in range(DENSE_SUB):
                if (i, grp) not in k_cache:
                    off = pl.multiple_of((step * DENSE_SUB + i) * TK, TK)
                    k_cache[(i, grp)] = kl_ref[pl.ds(off, TK), _lanes(grp)]
                dst[c, i] = _nt_dot(k_cache[(i, grp)], query(c))

        return produce

    def accumulate_from(chain_scores, v_tile, n_tiles, first):
        v_cache = {}

        def consume(c):
            hd = chains[c][1]
            m_new, alpha, probs = _softmax_tiles(chain_scores(c), None if first else m_scr[c])
            for i in range(n_tiles):
                if (i, hd) not in v_cache:
                    v_cache[(i, hd)] = v_tile(i, hd)
            pv = _pv([v_cache[(i, hd)] for i in range(n_tiles)], probs)
            acc_scr[c] = pv if first else acc_scr[c] * alpha + pv
            m_scr[c] = m_new

        return consume

    ctx_keys = [kc_ref[:, _lanes(grp)] for grp in range(1 + max(grp for grp, _ in chains))]
    ctx_scores = [[_nt_dot(ctx_keys[grp], query(c))] for c, (grp, _) in enumerate(chains)]
    _interleave(n_chains, score_into(sa_scr, 0),
                accumulate_from(lambda c: ctx_scores[c], lambda i, hd: vc_ref[0, _vrows(hd), :], 1, True),
                1)

    def step(t, src, dst):
        nxt = jnp.minimum(t + 1, n_steps - 1)
        _interleave(n_chains, score_into(dst, nxt),
                    accumulate_from(lambda c: [src[c, i] for i in range(DENSE_SUB)],
                                    lambda i, hd: vl_ref[t * DENSE_SUB + i, _vrows(hd), :],
                                    DENSE_SUB, False),
                    1)

    def body(t, carry):
        @pl.when((t & 1) == 0)
        def _():
            step(t, sa_scr, sb_scr)

        @pl.when((t & 1) == 1)
        def _():
            step(t, sb_scr, sa_scr)

        return carry

    lax.fori_loop(0, n_live, body, 0)

    if diff:
        dl = lamp_ref[...]
        lam = (jnp.exp(jnp.sum(dl[0:1] * dl[1:2], axis=1, keepdims=True))
               - jnp.exp(jnp.sum(dl[2:3] * dl[3:4], axis=1, keepdims=True)) + lam_init)
    for hd in range(N_HEADS):
        if diff:
            o = _normalised(acc_scr[2 * hd]) - lam * _normalised(acc_scr[2 * hd + 1])
            o = o * lax.rsqrt(jnp.mean(o * o, axis=0, keepdims=True) + SUBLN_EPS)
            o = o * subln_ref[...] * (1.0 - lam_init)
        else:
            o = _normalised(acc_scr[hd])
        ot_scr[hd * HEAD_V:(hd + 1) * HEAD_V, :] = o
    o_ref[...] = ot_scr[...].T.astype(BF16)


def _dense_attn(q, k, vt, *, seq, n_batch, diff, lam_init=0.0, lamp=None, subln=None):
    t, qw = q.shape
    kw = k.shape[1]
    nq = seq // TQ
    n_lat_blocks = n_batch * nq
    nv = vt.shape[1]
    n_chains = 2 * N_HEADS if diff else N_HEADS
    assert nq % DENSE_SUB == 0
    qmap = _q_block_map(nq, n_batch)
    in_specs = [pl.BlockSpec((TQ, qw), qmap),
                pl.BlockSpec((TK, kw), lambda b, j: (n_lat_blocks + b, 0)),
                pl.BlockSpec((seq, kw), lambda b, j: (b, 0)),
                pl.BlockSpec((1, nv, TK), lambda b, j: (n_lat_blocks + b, 0, 0)),
                pl.BlockSpec((nq, nv, TK), lambda b, j: (b, 0, 0))]
    args = [q, k, k, vt, vt]
    if diff:
        in_specs = [pl.BlockSpec(lamp.shape, lambda b, j: (0, 0)),
                    pl.BlockSpec(subln.shape, lambda b, j: (0, 0))] + in_specs
        args = [lamp, subln] + args
    return pl.pallas_call(
        functools.partial(_dense_attn_kernel, diff=diff, nq=nq, lam_init=lam_init),
        grid=(n_batch, nq + 1),
        in_specs=in_specs,
        out_specs=pl.BlockSpec((TQ, BRANCH_W), qmap),
        out_shape=jax.ShapeDtypeStruct((t, BRANCH_W), BF16),
        scratch_shapes=([pltpu.VMEM((BRANCH_W, TQ), F32),
                         pltpu.VMEM((n_chains, 1, TQ), F32),
                         pltpu.VMEM((n_chains, VROWS, TQ), F32),
                         pltpu.VMEM((n_chains, DENSE_SUB, TK, TQ), F32),
                         pltpu.VMEM((n_chains, DENSE_SUB, TK, TQ), F32)]
                        + ([pltpu.VMEM((n_chains, TQ, LANES), BF16)] if diff else [])),
        compiler_params=_cparams(2),
        name="diff_attn" if diff else "mla_attn",
    )(*args)


NA_Q_ROWS = TQ // GRID_W
NA_K_TILES = 3


def _na_attn_kernel(q_ref, kc_ref, k0_ref, k1_ref, k2_ref, vc_ref, v0_ref, v1_ref, v2_ref, bias_ref,
                    o_ref, ot_scr):
    k_refs = (kc_ref, k0_ref, k1_ref, k2_ref)
    v_refs = (vc_ref, v0_ref, v1_ref, v2_ref)
    keys = [[r[:, _lanes(grp)] for r in k_refs] for grp in range(N_HEADS // 2)]
    scores = {}

    def produce(hd):
        grp = hd // 2
        qm = _lane_masked(q_ref[:, _lanes(grp)], NA_HEAD * (hd % 2), NA_HEAD)
        sc = [_nt_dot(kt, qm) for kt in keys[grp]]
        scores[hd] = [sc[0]] + [sc[1 + t] + bias_ref[0, hd, t * TK:(t + 1) * TK, :]
                                for t in range(NA_K_TILES)]

    def consume(hd):
        _, _, probs = _softmax_tiles(scores[hd], None)
        acc = _pv([r[0, _vrows(hd), :] for r in v_refs], probs)
        ot_scr[hd * HEAD_V:(hd + 1) * HEAD_V, :] = _normalised(acc)

    _interleave(N_HEADS, produce, consume, N_HEADS)
    o_ref[...] = ot_scr[...].T.astype(BF16)


def _na_bias_tables(rpb, rows):
    n_heads = rpb.shape[0]
    exact = lax.Precision.HIGHEST
    kc = np.arange(GRID_W)[:, None]
    c = np.arange(GRID_W)[None, :]
    cs = np.clip(c - NA_COLS // 2, 0, GRID_W - NA_COLS)
    col_ok = (kc >= cs) & (kc < cs + NA_COLS)
    dc = np.clip(kc - c + NA_COLS - 1, 0, 2 * NA_COLS - 2)
    pick_dc = (dc[..., None] == np.arange(2 * NA_COLS - 1)).astype(np.float32)
    toe = jnp.einsum("hde,xce->hdxc", rpb.astype(F32), pick_dc, precision=exact)
    tabs = []
    for r0 in (0, NA_Q_ROWS, rows - NA_Q_ROWS):
        base = int(np.clip(r0 - NA_ROWS // 2, 0, rows - NA_K_TILES * NA_Q_ROWS))
        kr = (base + np.arange(NA_K_TILES * NA_Q_ROWS))[:, None]
        r = (r0 + np.arange(NA_Q_ROWS))[None, :]
        rs = np.clip(r - NA_ROWS // 2, 0, rows - NA_ROWS)
        row_ok = (kr >= rs) & (kr < rs + NA_ROWS)
        dr = np.clip(kr - r + NA_ROWS - 1, 0, 2 * NA_ROWS - 2)
        pick_dr = (dr[..., None] == np.arange(2 * NA_ROWS - 1)).astype(np.float32)
        bias = jnp.einsum("krd,hdxc->hkxrc", pick_dr, toe, precision=exact)
        ok = row_ok[:, None, :, None] & col_ok[None, :, None, :]
        tabs.append(jnp.where(jnp.asarray(ok)[None], bias * LOG2E, NEG_INF).reshape(n_heads, NA_K_TILES * TK, TQ))
    tabs.append(jnp.full_like(tabs[0], NEG_INF))
    return jnp.stack(tabs)


def _na_attn(q, k, vt, bias, *, seq, n_batch):
    t, w = q.shape
    nq = seq // TQ
    n_lat_blocks = n_batch * nq
    nv = vt.shape[1]
    qmap = _q_block_map(nq, n_batch)

    def kblock(tile):
        return lambda b, j: (b * nq + jnp.clip(j - 1, 0, nq - NA_K_TILES) + tile, 0)

    def vblock(tile):
        return lambda b, j: (b * nq + jnp.clip(j - 1, 0, nq - NA_K_TILES) + tile, 0, 0)

    variant = lambda b, j: (jnp.where(j == 0, 0, jnp.where(j == nq - 1, 2, jnp.where(j == nq, 3, 1))),
                            0, 0, 0)
    in_specs = ([pl.BlockSpec((TQ, w), qmap),
                 pl.BlockSpec((TK, w), lambda b, j: (n_lat_blocks + b, 0))]
                + [pl.BlockSpec((TK, w), kblock(i)) for i in range(NA_K_TILES)]
                + [pl.BlockSpec((1, nv, TK), lambda b, j: (n_lat_blocks + b, 0, 0))]
                + [pl.BlockSpec((1, nv, TK), vblock(i)) for i in range(NA_K_TILES)]
                + [pl.BlockSpec((1,) + bias.shape[1:], variant)])
    return pl.pallas_call(
        _na_attn_kernel,
        grid=(n_batch, nq + 1),
        in_specs=in_specs,
        out_specs=pl.BlockSpec((TQ, BRANCH_W), qmap),
        out_shape=jax.ShapeDtypeStruct((t, BRANCH_W), BF16),
        scratch_shapes=[pltpu.VMEM((BRANCH_W, TQ), F32)],
        compiler_params=_cparams(2),
        name="na_attn",
    )(q, k, k, k, k, vt, vt, vt, vt, bias)


def _gqa_attn_kernel(sink_ref, q_ref, kc_ref, k0_ref, k1_ref, k2_ref, vc_ref, v0_ref, v1_ref, v2_ref,
                     o_ref, ot_scr, *, nq):
    j = pl.program_id(1)
    k_refs = (kc_ref, k0_ref, k1_ref, k2_ref)
    v_refs = (vc_ref, v0_ref, v1_ref, v2_ref)
    key_i = lax.broadcasted_iota(jnp.int32, (TK, TQ), 0)
    qry_i = lax.broadcasted_iota(jnp.int32, (TK, TQ), 1)
    keeps = []
    for rel in (-1, 0, 1):
        tile = j + rel
        tile_ok = (tile >= 0) & (tile < nq) & (j < nq)
        reach = jnp.where(tile_ok, WINDOW, -1)
        keeps.append(jnp.abs(rel * TK + key_i - qry_i) <= reach)
    heads_per_kv = N_HEADS // GQA_KV_HEADS
    keys = [[r[:, _lanes(kv)] for r in k_refs] for kv in range(GQA_KV_HEADS)]
    scores = {}

    def produce(hd):
        kv = hd // heads_per_kv
        qm = _lane_masked(q_ref[:, _lanes(kv)], GQA_HEAD * (hd % heads_per_kv), GQA_HEAD)
        sc = [_nt_dot(kt, qm) for kt in keys[kv]]
        scores[hd] = [sc[0]] + [jnp.where(keeps[t], sc[1 + t], NEG_INF) for t in range(3)]

    def consume(hd):
        sink = jnp.full((1, TQ), sink_ref[hd] * LOG2E, F32)
        _, sink_weight, probs = _softmax_tiles(scores[hd], sink)
        acc = _pv([r[0, _vrows(hd // heads_per_kv), :] for r in v_refs], probs)
        ot_scr[hd * HEAD_V:(hd + 1) * HEAD_V, :] = _normalised(acc, sink_weight)

    _interleave(N_HEADS, produce, consume, N_HEADS)
    o_ref[...] = ot_scr[...].T.astype(BF16)


def _gqa_attn(q, k, vt, sink, *, seq, n_batch):
    t, w = q.shape
    nq = seq // TQ
    n_lat_blocks = n_batch * nq
    nv = vt.shape[1]
    qmap = _q_block_map(nq, n_batch)

    def kblock(rel):
        return lambda b, j: (b * nq + jnp.clip(j + rel, 0, nq - 1), 0)

    def vblock(rel):
        return lambda b, j: (b * nq + jnp.clip(j + rel, 0, nq - 1), 0, 0)

    in_specs = ([pl.BlockSpec(memory_space=pltpu.SMEM),
                 pl.BlockSpec((TQ, w), qmap),
                 pl.BlockSpec((TK, w), lambda b, j: (n_lat_blocks + b, 0))]
                + [pl.BlockSpec((TK, w), kblock(rel)) for rel in (-1, 0, 1)]
                + [pl.BlockSpec((1, nv, TK), lambda b, j: (n_lat_blocks + b, 0, 0))]
                + [pl.BlockSpec((1, nv, TK), vblock(rel)) for rel in (-1, 0, 1)])
    return pl.pallas_call(
        functools.partial(_gqa_attn_kernel, nq=nq),
        grid=(n_batch, nq + 1),
        in_specs=in_specs,
        out_specs=pl.BlockSpec((TQ, BRANCH_W), qmap),
        out_shape=jax.ShapeDtypeStruct((t, BRANCH_W), BF16),
        scratch_shapes=[pltpu.VMEM((BRANCH_W, TQ), F32)],
        compiler_params=_cparams(2),
        name="gqa_attn",
    )(sink, q, k, k, k, k, vt, vt, vt, vt)


def _merge_kernel(h_ref, mod_ref, g_ref, ya_ref, yb_ref, yc_ref, yd_ref, wg_ref, bg_ref, wb_ref, wo_ref,
                  o_ref):
    h = h_ref[...]
    mod = mod_ref[0]
    u = _modulate(h, g_ref[...], mod[3:4], mod[4:5]).astype(BF16)
    merged = None
    for i, y_ref in enumerate((ya_ref, yb_ref, yc_ref, yd_ref)):
        gate = jax.nn.sigmoid(jnp.dot(u, wg_ref[i], preferred_element_type=F32) + bg_ref[i])
        term = gate * jnp.dot(y_ref[...], wb_ref[i], preferred_element_type=F32)
        merged = term if merged is None else merged + term
    y = jnp.dot(merged.astype(BF16), wo_ref[...], preferred_element_type=F32)
    o_ref[...] = h + mod[5:6] * y


def _merge(h, mod_l, g, ys, wg, bg, wb, wo, *, seq, n_batch):
    t, d = h.shape
    tok = lambda i: (i, 0)
    const2 = lambda i: (0, 0)
    const3 = lambda i: (0, 0, 0)
    in_specs = ([pl.BlockSpec((TM, d), tok),
                 pl.BlockSpec((1, N_MOD, d), lambda i: (jnp.minimum(i * TM // seq, n_batch), 0, 0)),
                 pl.BlockSpec((1, d), const2)]
                + [pl.BlockSpec((TM, BRANCH_W), tok)] * 4
                + [pl.BlockSpec(wg.shape, const3), pl.BlockSpec(bg.shape, const3),
                   pl.BlockSpec(wb.shape, const3), pl.BlockSpec(wo.shape, const2)])
    return pl.pallas_call(
        _merge_kernel,
        grid=(t // TM,),
        in_specs=in_specs,
        out_specs=pl.BlockSpec((TM, d), tok),
        out_shape=jax.ShapeDtypeStruct((t, d), F32),
        compiler_params=_cparams(1),
        name="merge",
    )(h, mod_l, g, *ys, wg, bg, wb, wo)


def _rope_tables(seq, dim):
    tpos = jnp.arange(seq)
    rows = (tpos // GRID_W).astype(F32)
    cols = (tpos % GRID_W).astype(F32)
    half = dim // 2
    freqs = jnp.power(ROPE_BASE, -jnp.arange(0, half, 2, dtype=F32) / half)
    ang = jnp.concatenate([rows[:, None] * freqs] * 2 + [cols[:, None] * freqs] * 2, axis=1)
    nfreq = dim // 4
    sign = np.where((np.arange(dim) % (2 * nfreq)) < nfreq, -1.0, 1.0).astype(np.float32)
    reps = LANES // dim
    cos = jnp.tile(jnp.cos(ang), (1, reps))
    sin = jnp.tile(jnp.sin(ang) * sign, (1, reps))
    cos = jnp.concatenate([cos, jnp.ones((TM, LANES), F32)], axis=0)
    sin = jnp.concatenate([sin, jnp.zeros((TM, LANES), F32)], axis=0)
    return cos, sin


def _vt_weight(w_cols, n_heads):
    d_in = w_cols.shape[0]
    wt = w_cols.T.reshape(n_heads, HEAD_V, d_in)
    wt = jnp.pad(wt, ((0, 0), (0, VROWS - HEAD_V), (0, 0))).reshape(n_heads * VROWS, d_in)
    bias = np.zeros((n_heads, VROWS, 1), np.float32)
    bias[:, ONES_ROW] = 1.0
    return wt, jnp.asarray(bias.reshape(n_heads * VROWS, 1))


def _mixin_weights(w_in, q_norm, w_uq, kv_norm, w_ukv):
    offs = np.concatenate([[0], np.cumsum(IN_SIZES)])
    col = lambda i: w_in[:, offs[i]:offs[i + 1]]
    d = w_in.shape[0]
    kr = jnp.zeros((d, LANES), F32).at[:, MLA_NOPE:MLA_NOPE + MLA_ROPE].set(col(2))
    gk = col(10).reshape(d, GQA_KV_HEADS, 1, GQA_HEAD)
    gk2 = jnp.broadcast_to(gk, (d, GQA_KV_HEADS, 2, GQA_HEAD)).reshape(d, 2 * GQA_KV_HEADS * GQA_HEAD)
    wmain = jnp.concatenate([col(0), col(1), kr, col(3) * (DIFF_SCALE * LOG2E), col(4),
                             col(6) * (NA_SCALE * LOG2E), col(7), col(9) * (GQA_SCALE * LOG2E), gk2], axis=1)
    wt_b, bt_b = _vt_weight(col(5), N_HEADS)
    wt_c, bt_c = _vt_weight(col(8), N_HEADS)
    wt_d, bt_d = _vt_weight(col(11), GQA_KV_HEADS)
    qk = MLA_NOPE + MLA_ROPE
    wq2 = jnp.pad((w_uq * (MLA_SCALE * LOG2E)).reshape(MLA_Q_RANK, N_HEADS, qk),
                  ((0, 0), (0, 0), (0, LANES - qk))).reshape(MLA_Q_RANK, N_HEADS * LANES)
    ukv = w_ukv.reshape(MLA_KV_RANK, N_HEADS, MLA_NOPE + HEAD_V)
    wk2 = jnp.pad(ukv[:, :, :MLA_NOPE], ((0, 0), (0, 0), (0, LANES - MLA_NOPE))
                  ).reshape(MLA_KV_RANK, N_HEADS * LANES)
    wv2, bv2 = _vt_weight(ukv[:, :, MLA_NOPE:].reshape(MLA_KV_RANK, N_HEADS * HEAD_V), N_HEADS)
    return {"wmain": wmain.astype(BF16),
            "wt": jnp.concatenate([wt_b, wt_c, wt_d], axis=0).astype(BF16),
            "bt": jnp.concatenate([bt_b, bt_c, bt_d], axis=0),
            "qn": q_norm.reshape(1, -1), "kvn": kv_norm.reshape(1, -1),
            "wq2": wq2.astype(BF16), "wk2": wk2.astype(BF16), "wv2": wv2.astype(BF16), "bv2": bv2}


def _ffn_weights(w_gu, w_down):
    d, two_f = w_gu.shape
    d_ff = two_f // 2
    n_chunks = d_ff // FF_CHUNK
    gate = w_gu[:, :d_ff].reshape(d, n_chunks, FF_CHUNK)
    up = w_gu[:, d_ff:].reshape(d, n_chunks, FF_CHUNK)
    wgu_c = jnp.transpose(jnp.concatenate([gate, up], axis=2), (1, 0, 2)).astype(BF16)
    return wgu_c, w_down.astype(BF16)


def kernel(x, c, ctx, c_ctx, w_ada, b_ada, norm_ffn1, ffn1_w_gu, ffn1_w_down, norm_mix, w_in, mla_q_norm,
           mla_w_uq, mla_kv_norm, mla_w_ukv, diff_lam, diff_subln, na_rpb, gqa_sink, w_branch, w_gate,
           b_gate, w_out, norm_ffn2, ffn2_w_gu, ffn2_w_down, final_norm):
    n_batch, seq, d = x.shape
    ctx_len = ctx.shape[1]
    depth = w_ada.shape[0]
    rows = seq // GRID_W
    assert ctx_len == TK and seq % TM == 0 and (n_batch * ctx_len) % TM == 0
    assert rows % NA_Q_ROWS == 0 and rows >= NA_K_TILES * NA_Q_ROWS and seq // TQ >= 3
    n_lat = n_batch * seq
    dims = dict(seq=seq, n_batch=n_batch)

    r_pad = -(-(n_batch + 1) // 8) * 8
    cc = jnp.zeros((r_pad, d), F32).at[:n_batch].set(c).at[n_batch].set(c_ctx)
    mod = _modulation(cc, w_ada, b_ada).reshape(depth, r_pad, N_MOD, d)

    tabs = _rope_tables(seq, MLA_ROPE) + _rope_tables(seq, GQA_HEAD)
    h = jnp.concatenate([x.reshape(n_lat, d), ctx.reshape(n_batch * ctx_len, d)], axis=0)
    n_tok = h.shape[0]

    for l in range(depth):
        last = l == depth - 1
        lam_init = 0.8 - 0.6 * math.exp(-0.3 * l)
        mod_l = mod[l]
        row = lambda v: v.reshape(1, -1)
        wgu1, wdn1 = _ffn_weights(ffn1_w_gu[l], ffn1_w_down[l])
        wgu2, wdn2 = _ffn_weights(ffn2_w_gu[l], ffn2_w_down[l])

        h = _ffn(h, mod_l, row(norm_ffn1[l]), wgu1, wdn1, mod_base=0, n_tok_out=n_tok, **dims)

        mw = _mixin_weights(w_in[l], mla_q_norm[l], mla_w_uq[l], mla_kv_norm[l], mla_w_ukv[l])
        (qa, ka, vat, qb, kb, vbt, qc, kc, vct, qd, kd, vdt) = _mixin(
            h, mod_l, row(norm_mix[l]), tabs, mw, **dims)
        ya = _dense_attn(qa, ka, vat, diff=False, **dims)
        yb = _dense_attn(qb, kb, vbt, diff=True, lam_init=lam_init, lamp=diff_lam[l].astype(F32),
                         subln=diff_subln[l].reshape(HEAD_V, 1), **dims)
        yc = _na_attn(qc, kc, vct, _na_bias_tables(na_rpb[l], rows), **dims)
        yd = _gqa_attn(qd, kd, vdt, gqa_sink[l].astype(F32), **dims)
        h = _merge(h, mod_l, row(norm_mix[l]), (ya, yb, yc, yd), w_gate[l].astype(BF16),
                   b_gate[l].reshape(4, 1, d), w_branch[l].astype(BF16), w_out[l].astype(BF16), **dims)

        h = _ffn(h, mod_l, row(norm_ffn2[l]), wgu2, wdn2, mod_base=6,
                 n_tok_out=n_lat if last else n_tok, final_g=row(final_norm) if last else None, **dims)
    return h.reshape(n_batch, seq, d)
```

```python
import functools
import math

import numpy as np
import jax
import jax.numpy as jnp
from jax import lax
from jax.experimental import pallas as pl
from jax.experimental.pallas import tpu as pltpu

F32 = jnp.float32
BF16 = jnp.bfloat16

GRID_W = 64
N_MOD = 9
ROPE_BASE = 10000.0
NEG_INF = -1e30
LOG2E = math.log2(math.e)
EPS = 1e-6
SUBLN_EPS = 1e-5
N_HEADS = 4
HEAD_V = 64
MLA_Q_RANK = 256
MLA_KV_RANK = 128
MLA_NOPE = 64
MLA_ROPE = 32
MLA_SCALE = (MLA_NOPE + MLA_ROPE) ** -0.5
DIFF_HEAD = 32
DIFF_SCALE = DIFF_HEAD ** -0.5
NA_HEAD = 64
NA_ROWS = 8
NA_COLS = 16
NA_SCALE = NA_HEAD ** -0.5
GQA_KV_HEADS = 2
GQA_HEAD = 64
WINDOW = 128
GQA_SCALE = GQA_HEAD ** -0.5
BRANCH_W = 256
IN_SIZES = (256, 128, 32, 256, 256, 256, 256, 256, 256, 256, 128, 128)

LANES = 128
TQ = 256
TK = 256
MLA_SUB = 8
DIFF_SUB = 4
VROWS = 80
ONES_ROW = HEAD_V
TM = 512
FF_CHUNK = 256
VMEM_LIMIT = 56 * 1024 * 1024


def _cparams(n_axes):
    return pltpu.CompilerParams(dimension_semantics=("arbitrary",) * n_axes,
                                vmem_limit_bytes=VMEM_LIMIT)


def _rms(x, eps):
    return x * lax.rsqrt(jnp.mean(x * x, axis=-1, keepdims=True) + eps)


def _modulate(h, g, shift, scale):
    return (_rms(h, EPS) * g) * (1.0 + scale) + shift


def _nt_dot(a, b):
    return lax.dot_general(a, b, (((1,), (1,)), ((), ())), preferred_element_type=F32)


def _mod_kernel(s_ref, w_ref, b_ref, o_ref):
    s = s_ref[...]
    s = s * jax.nn.sigmoid(s)
    o_ref[0] = jnp.dot(s, w_ref[0], preferred_element_type=F32,
                       precision=lax.Precision.HIGHEST) + b_ref[0]


def _modulation(cc, w_ada, b_ada):
    depth, d, nd = w_ada.shape
    r = cc.shape[0]
    return pl.pallas_call(
        _mod_kernel,
        grid=(depth, nd // d),
        in_specs=[pl.BlockSpec((r, d), lambda l, n: (0, 0)),
                  pl.BlockSpec((1, d, d), lambda l, n: (l, 0, n)),
                  pl.BlockSpec((1, 1, d), lambda l, n: (l, 0, n))],
        out_specs=pl.BlockSpec((1, r, d), lambda l, n: (l, 0, n)),
        out_shape=jax.ShapeDtypeStruct((depth, r, nd), F32),
        compiler_params=_cparams(2),
        name="adaln_mod",
    )(cc, w_ada, b_ada.reshape(depth, 1, nd))


def _ffn_kernel(*refs, mod_base, final, n_lat_tiles):
    if n_lat_tiles is not None:
        x_ref, c_ref, *refs = refs
        h = jnp.where(pl.program_id(0) < n_lat_tiles, x_ref[...], c_ref[...])
    else:
        h_ref, *refs = refs
        h = h_ref[...]
    if final:
        mod_ref, g_ref, wgu_ref, wdn_ref, fn_ref, o_ref, a_scr = refs
    else:
        mod_ref, g_ref, wgu_ref, wdn_ref, o_ref, a_scr = refs
    mod = mod_ref[0]
    shift = mod[mod_base:mod_base + 1]
    scale = mod[mod_base + 1:mod_base + 2]
    gate = mod[mod_base + 2:mod_base + 3]
    u = _modulate(h, g_ref[...], shift, scale).astype(BF16)
    for j in range(wgu_ref.shape[0]):
        gu = jnp.dot(u, wgu_ref[j], preferred_element_type=F32)
        g = gu[:, :FF_CHUNK]
        up = gu[:, FF_CHUNK:]
        a_scr[:, j * FF_CHUNK:(j + 1) * FF_CHUNK] = (g * jax.nn.sigmoid(g) * up).astype(BF16)
    y = jnp.dot(a_scr[...], wdn_ref[...], preferred_element_type=F32)
    out = h + (0.5 * gate) * y
    if final:
        out = _rms(out, EPS) * fn_ref[...]
    o_ref[...] = out


def _ffn(h, mod_l, g, wgu_c, wdn, *, mod_base, n_tok_out, seq, n_batch, final_g=None):
    n_chunks, d = wgu_c.shape[:2]
    d_ff = wdn.shape[0]
    final = final_g is not None
    const2 = lambda i: (0, 0)
    if isinstance(h, tuple):
        n_lat_tiles = h[0].shape[0] // TM
        tokens = list(h)
        in_specs = [pl.BlockSpec((TM, d), lambda i: (jnp.minimum(i, n_lat_tiles - 1), 0)),
                    pl.BlockSpec((TM, d), lambda i: (jnp.maximum(i - n_lat_tiles, 0), 0))]
    else:
        n_lat_tiles = None
        tokens = [h]
        in_specs = [pl.BlockSpec((TM, d), lambda i: (i, 0))]
    in_specs += [pl.BlockSpec((1, N_MOD, d), lambda i: (jnp.minimum(i * TM // seq, n_batch), 0, 0)),
                 pl.BlockSpec((1, d), const2),
                 pl.BlockSpec((n_chunks, d, 2 * FF_CHUNK), lambda i: (0, 0, 0)),
                 pl.BlockSpec((d_ff, d), const2)]
    args = tokens + [mod_l, g, wgu_c, wdn]
    if final:
        in_specs.append(pl.BlockSpec((1, d), const2))
        args.append(final_g)
    return pl.pallas_call(
        functools.partial(_ffn_kernel, mod_base=mod_base, final=final, n_lat_tiles=n_lat_tiles),
        grid=(n_tok_out // TM,),
        in_specs=in_specs,
        out_specs=pl.BlockSpec((TM, d), lambda i: (i, 0)),
        out_shape=jax.ShapeDtypeStruct((n_tok_out, d), F32),
        scratch_shapes=[pltpu.VMEM((TM, d_ff), BF16)],
        compiler_params=_cparams(1),
        name="ffn",
    )(*args)


def _rope(x, cos, sgn_sin, half):
    lane = lax.broadcasted_iota(jnp.int32, (1, LANES), 1)
    first = (lane & (2 * half - 1)) < half
    outs = []
    for g in range(x.shape[1] // LANES):
        xg = x[:, g * LANES:(g + 1) * LANES]
        partner = jnp.where(first, pltpu.roll(xg, LANES - half, 1), pltpu.roll(xg, half, 1))
        outs.append(xg * cos + partner * sgn_sin)
    return outs


def _mixin_kernel(h_ref, mod_ref, g_ref, c32_ref, s32_ref, c64_ref, s64_ref,
                  wmain_ref, wt_ref, bt_ref, qn_ref, kvn_ref, wq2_ref, wk2_ref, wv2_ref, bv2_ref,
                  qa_ref, ka_ref, vat_ref, qb_ref, kb_ref, vbt_ref, qc_ref, kc_ref, vct_ref,
                  qd_ref, kd_ref, vdt_ref):
    mod = mod_ref[0]
    u = _modulate(h_ref[...], g_ref[...], mod[3:4], mod[4:5]).astype(BF16)
    pm = jnp.dot(u, wmain_ref[...], preferred_element_type=F32)
    c32, s32, c64, s64 = c32_ref[...], s32_ref[...], c64_ref[...], s64_ref[...]
    lane = lax.broadcasted_iota(jnp.int32, (1, LANES), 1)
    rope_lanes = (lane >= MLA_NOPE) & (lane < MLA_NOPE + MLA_ROPE)
    c_mla = jnp.where(rope_lanes, c32, 1.0)
    s_mla = jnp.where(rope_lanes, s32, 0.0)

    cqn = (_rms(pm[:, 0:256], EPS) * qn_ref[...]).astype(BF16)
    ckvn = (_rms(pm[:, 256:384], EPS) * kvn_ref[...]).astype(BF16)
    qa = _rope(jnp.dot(cqn, wq2_ref[...], preferred_element_type=F32), c_mla, s_mla, MLA_ROPE // 4)
    k_rope = _rope(pm[:, 384:512], c_mla, s_mla, MLA_ROPE // 4)[0]
    ka = jnp.dot(ckvn, wk2_ref[...], preferred_element_type=F32)
    for hd in range(N_HEADS):
        sl = slice(hd * LANES, (hd + 1) * LANES)
        qa_ref[:, sl] = qa[hd].astype(BF16)
        ka_ref[:, sl] = (ka[:, sl] + k_rope).astype(BF16)
    vat = _nt_dot(wv2_ref[...], ckvn) + bv2_ref[...]

    for idx, (ref, lo, cos, sin, half) in enumerate((
            (qb_ref, 512, c32, s32, DIFF_HEAD // 4), (kb_ref, 768, c32, s32, DIFF_HEAD // 4),
            (qd_ref, 1536, c64, s64, GQA_HEAD // 4), (kd_ref, 1792, c64, s64, GQA_HEAD // 4))):
        parts = _rope(pm[:, lo:lo + 256], cos, sin, half)
        for g in range(2):
            ref[:, g * LANES:(g + 1) * LANES] = parts[g].astype(BF16)
    qc_ref[...] = pm[:, 1024:1280].astype(BF16)
    kc_ref[...] = pm[:, 1280:1536].astype(BF16)

    vt = _nt_dot(wt_ref[...], u) + bt_ref[...]
    nv = N_HEADS * VROWS
    for c in range(TM // TK):
        cs = slice(c * TK, (c + 1) * TK)
        vat_ref[c] = vat[:, cs].astype(BF16)
        vbt_ref[c] = vt[0:nv, cs].astype(BF16)
        vct_ref[c] = vt[nv:2 * nv, cs].astype(BF16)
        vdt_ref[c] = vt[2 * nv:, cs].astype(BF16)


def _mixin(h, mod_l, g, tabs, w, *, seq, n_batch):
    t, d = h.shape
    n_lat_tiles = n_batch * seq // TM
    tiles_per_seq = seq // TM
    tok = lambda i: (i, 0)
    const2 = lambda i: (0, 0)
    tab_map = lambda i: (jnp.where(i < n_lat_tiles, i % tiles_per_seq, tiles_per_seq), 0)
    vt_map = lambda i: (i, 0, 0)
    nv = N_HEADS * VROWS
    nvd = GQA_KV_HEADS * VROWS
    full = lambda a: pl.BlockSpec(a.shape, const2)
    in_specs = [pl.BlockSpec((TM, d), tok),
                pl.BlockSpec((1, N_MOD, d), lambda i: (jnp.minimum(i * TM // seq, n_batch), 0, 0)),
                pl.BlockSpec((1, d), const2)]
    in_specs += [pl.BlockSpec((TM, LANES), tab_map)] * 4
    wlist = [w["wmain"], w["wt"], w["bt"], w["qn"], w["kvn"], w["wq2"], w["wk2"], w["wv2"], w["bv2"]]
    in_specs += [full(a) for a in wlist]
    nat = lambda width: (pl.BlockSpec((TM, width), tok), jax.ShapeDtypeStruct((t, width), BF16))
    tr = lambda rows: (pl.BlockSpec((TM // TK, rows, TK), vt_map),
                       jax.ShapeDtypeStruct((t // TK, rows, TK), BF16))
    outs = [nat(512), nat(512), tr(nv), nat(256), nat(256), tr(nv), nat(256), nat(256), tr(nv),
            nat(256), nat(256), tr(nvd)]
    return pl.pallas_call(
        _mixin_kernel,
        grid=(t // TM,),
        in_specs=in_specs,
        out_specs=[o[0] for o in outs],
        out_shape=[o[1] for o in outs],
        compiler_params=_cparams(1),
        name="mixer_in",
    )(h, mod_l, g, *tabs, *wlist)


def _lane_masked(qg, lo, width):
    lane = lax.broadcasted_iota(jnp.int32, (1, LANES), 1)
    return jnp.where((lane >= lo) & (lane < lo + width), qg, jnp.zeros_like(qg))


def _transposed(q):
    return q.astype(F32).T.astype(BF16)


def _interleave(n_chains, produce, consume, lag):
    for idx in range(n_chains + lag):
        if idx < n_chains:
            produce(idx)
        if idx >= lag:
            consume(idx - lag)


def _softmax_tiles(scores, m_prev, tile_max=None):
    m_new = m_prev
    if tile_max is None:
        tile_max = [jnp.max(s, axis=0, keepdims=True) for s in scores]
    for mt in tile_max:
        m_new = mt if m_new is None else jnp.maximum(m_new, mt)
    probs = [jnp.exp2(s - m_new).astype(BF16) for s in scores]
    alpha = None if m_prev is None else jnp.exp2(m_prev - m_new)
    return m_new, alpha, probs


def _pv(v_tiles, probs):
    out = None
    for vt, p in zip(v_tiles, probs):
        d = jnp.dot(vt, p, preferred_element_type=F32)
        out = d if out is None else out + d
    return out


def _normalised(acc, extra_denominator=None):
    den = acc[ONES_ROW:ONES_ROW + 1]
    if extra_denominator is not None:
        den = den + extra_denominator
    return acc[0:HEAD_V] / den


def _lanes(grp):
    return slice(grp * LANES, (grp + 1) * LANES)


def _vrows(hd):
    return slice(hd * VROWS, (hd + 1) * VROWS)


def _q_block_map(nq, n_batch):
    return lambda b, j: (jnp.where(j < nq, b * nq + j, n_batch * nq + b), 0)


def _dense_attn_kernel(*refs, diff, nq, lam_init):
    if diff:
        (lamp_ref, subln_ref, q_ref, kc_ref, kl_ref, vc_ref, vl_ref, o_ref,
         ot_scr, m_scr, acc_scr, sa_scr, sb_scr, ma_scr, mb_scr, qt_scr) = refs
        chains = [(hd // 2, hd) for hd in range(N_HEADS) for _ in range(2)]
        for c, (grp, hd) in enumerate(chains):
            lo = 2 * DIFF_HEAD * (hd % 2) + DIFF_HEAD * (c % 2)
            qt_scr[c] = _transposed(_lane_masked(q_ref[:, _lanes(grp)], lo, DIFF_HEAD))
    else:
        (q_ref, kc_ref, kl_ref, vc_ref, vl_ref, o_ref,
         ot_scr, m_scr, acc_scr, sa_scr, sb_scr, ma_scr, mb_scr, qt_scr) = refs
        chains = [(hd, hd) for hd in range(N_HEADS)]
        for c in range(N_HEADS):
            qt_scr[c] = _transposed(q_ref[:, _lanes(c)])
    j = pl.program_id(1)
    DENSE_SUB = sa_scr.shape[1]
    n_steps = nq // DENSE_SUB
    n_live = jnp.where(j < nq, n_steps, 0)
    n_chains = len(chains)

    def scores(kt, c):
        return jnp.dot(kt, qt_scr[c], preferred_element_type=F32)

    def score_into(dst, dst_max, step):
        k_cache = {}

        def produce(c):
            grp = chains[c][0]
            for i in range(DENSE_SUB):
                if (i, grp) not in k_cache:
                    off = pl.multiple_of((step * DENSE_SUB + i) * TK, TK)
                    k_cache[(i, grp)] = kl_ref[pl.ds(off, TK), _lanes(grp)]
                sc = scores(k_cache[(i, grp)], c)
                dst[c, i] = sc
                dst_max[c, i] = jnp.max(sc, axis=0, keepdims=True)

        return produce

    def accumulate_from(chain_scores, chain_max, v_tile, n_tiles, first):
        v_cache = {}

        def consume(c):
            hd = chains[c][1]
            m_new, alpha, probs = _softmax_tiles(chain_scores(c), None if first else m_scr[c],
                                                 chain_max(c))
            for i in range(n_tiles):
                if (i, hd) not in v_cache:
                    v_cache[(i, hd)] = v_tile(i, hd)
            pv = _pv([v_cache[(i, hd)] for i in range(n_tiles)], probs)
            acc_scr[c] = pv if first else acc_scr[c] * alpha + pv
            m_scr[c] = m_new

        return consume

    ctx_keys = [kc_ref[:, _lanes(grp)] for grp in range(1 + max(grp for grp, _ in chains))]
    ctx_scores = [[scores(ctx_keys[grp], c)] for c, (grp, _) in enumerate(chains)]
    _interleave(n_chains, score_into(sa_scr, ma_scr, 0),
                accumulate_from(lambda c: ctx_scores[c], lambda c: None,
                                lambda i, hd: vc_ref[0, _vrows(hd), :], 1, True),
                1)

    buffers = ((sa_scr, ma_scr), (sb_scr, mb_scr))
    last = n_steps - 1

    def step(t, parity, look_ahead):
        (src, src_max), (dst, dst_max) = buffers[parity], buffers[1 - parity]
        consume = accumulate_from(lambda c: [src[c, i] for i in range(DENSE_SUB)],
                                  lambda c: [src_max[c, i] for i in range(DENSE_SUB)],
                                  lambda i, hd: vl_ref[t * DENSE_SUB + i, _vrows(hd), :],
                                  DENSE_SUB, False)
        produce = score_into(dst, dst_max, t + 1) if look_ahead else (lambda c: None)
        _interleave(n_chains, produce, consume, 1)

    def body(t, carry):
        for parity in range(2):
            @pl.when(((t & 1) == parity) & (t != last))
            def _():
                step(t, parity, True)

        @pl.when(t == last)
        def _():
            step(t, last % 2, False)

        return carry

    lax.fori_loop(0, n_live, body, 0)

    if diff:
        dl = lamp_ref[...]
        lam = (jnp.exp(jnp.sum(dl[0:1] * dl[1:2], axis=1, keepdims=True))
               - jnp.exp(jnp.sum(dl[2:3] * dl[3:4], axis=1, keepdims=True)) + lam_init)
    for hd in range(N_HEADS):
        if diff:
            o = _normalised(acc_scr[2 * hd]) - lam * _normalised(acc_scr[2 * hd + 1])
            o = o * lax.rsqrt(jnp.mean(o * o, axis=0, keepdims=True) + SUBLN_EPS)
            o = o * subln_ref[...] * (1.0 - lam_init)
        else:
            o = _normalised(acc_scr[hd])
        ot_scr[hd * HEAD_V:(hd + 1) * HEAD_V, :] = o
    o_ref[...] = ot_scr[...].T.astype(BF16)


def _dense_attn(q, k, vt, *, seq, n_batch, diff, lam_init=0.0, lamp=None, subln=None):
    t, qw = q.shape
    kw = k.shape[1]
    nq = seq // TQ
    n_lat_blocks = n_batch * nq
    nv = vt.shape[1]
    n_chains = 2 * N_HEADS if diff else N_HEADS
    DENSE_SUB = DIFF_SUB if diff else MLA_SUB
    assert nq % DENSE_SUB == 0
    qmap = _q_block_map(nq, n_batch)
    in_specs = [pl.BlockSpec((TQ, qw), qmap),
                pl.BlockSpec((TK, kw), lambda b, j: (n_lat_blocks + b, 0)),
                pl.BlockSpec((seq, kw), lambda b, j: (b, 0)),
                pl.BlockSpec((1, nv, TK), lambda b, j: (n_lat_blocks + b, 0, 0)),
                pl.BlockSpec((nq, nv, TK), lambda b, j: (b, 0, 0))]
    args = [q, k, k, vt, vt]
    if diff:
        in_specs = [pl.BlockSpec(lamp.shape, lambda b, j: (0, 0)),
                    pl.BlockSpec(subln.shape, lambda b, j: (0, 0))] + in_specs
        args = [lamp, subln] + args
    return pl.pallas_call(
        functools.partial(_dense_attn_kernel, diff=diff, nq=nq, lam_init=lam_init),
        grid=(n_batch, nq + 1),
        in_specs=in_specs,
        out_specs=pl.BlockSpec((TQ, BRANCH_W), qmap),
        out_shape=jax.ShapeDtypeStruct((t, BRANCH_W), BF16),
        scratch_shapes=([pltpu.VMEM((BRANCH_W, TQ), F32),
                         pltpu.VMEM((n_chains, 1, TQ), F32),
                         pltpu.VMEM((n_chains, VROWS, TQ), F32),
                         pltpu.VMEM((n_chains, DENSE_SUB, TK, TQ), F32),
                         pltpu.VMEM((n_chains, DENSE_SUB, TK, TQ), F32),
                         pltpu.VMEM((n_chains, DENSE_SUB, 1, TQ), F32),
                         pltpu.VMEM((n_chains, DENSE_SUB, 1, TQ), F32),
                         pltpu.VMEM((n_chains, LANES, TQ), BF16)]),
        compiler_params=_cparams(2),
        name="diff_attn" if diff else "mla_attn",
    )(*args)


NA_Q_ROWS = TQ // GRID_W
NA_K_TILES = 3


def _na_attn_kernel(q_ref, kc_ref, k0_ref, k1_ref, k2_ref, vc_ref, v0_ref, v1_ref, v2_ref, bias_ref,
                    o_ref, ot_scr):
    k_refs = (kc_ref, k0_ref, k1_ref, k2_ref)
    v_refs = (vc_ref, v0_ref, v1_ref, v2_ref)
    keys = [[r[:, _lanes(grp)] for r in k_refs] for grp in range(N_HEADS // 2)]
    scores = {}

    def produce(hd):
        grp = hd // 2
        qm = _lane_masked(q_ref[:, _lanes(grp)], NA_HEAD * (hd % 2), NA_HEAD)
        sc = [_nt_dot(kt, qm) for kt in keys[grp]]
        scores[hd] = [sc[0]] + [sc[1 + t] + bias_ref[0, hd, t * TK:(t + 1) * TK, :]
                                for t in range(NA_K_TILES)]

    def consume(hd):
        _, _, probs = _softmax_tiles(scores[hd], None)
        acc = _pv([r[0, _vrows(hd), :] for r in v_refs], probs)
        ot_scr[hd * HEAD_V:(hd + 1) * HEAD_V, :] = _normalised(acc)

    _interleave(N_HEADS, produce, consume, N_HEADS)
    o_ref[...] = ot_scr[...].T.astype(BF16)


def _na_bias_tables(rpb, rows):
    n_heads = rpb.shape[0]
    exact = lax.Precision.HIGHEST
    kc = np.arange(GRID_W)[:, None]
    c = np.arange(GRID_W)[None, :]
    cs = np.clip(c - NA_COLS // 2, 0, GRID_W - NA_COLS)
    col_ok = (kc >= cs) & (kc < cs + NA_COLS)
    dc = np.clip(kc - c + NA_COLS - 1, 0, 2 * NA_COLS - 2)
    pick_dc = (dc[..., None] == np.arange(2 * NA_COLS - 1)).astype(np.float32)
    toe = jnp.einsum("hde,xce->hdxc", rpb.astype(F32), pick_dc, precision=exact)
    tabs = []
    for r0 in (0, NA_Q_ROWS, rows - NA_Q_ROWS):
        base = int(np.clip(r0 - NA_ROWS // 2, 0, rows - NA_K_TILES * NA_Q_ROWS))
        kr = (base + np.arange(NA_K_TILES * NA_Q_ROWS))[:, None]
        r = (r0 + np.arange(NA_Q_ROWS))[None, :]
        rs = np.clip(r - NA_ROWS // 2, 0, rows - NA_ROWS)
        row_ok = (kr >= rs) & (kr < rs + NA_ROWS)
        dr = np.clip(kr - r + NA_ROWS - 1, 0, 2 * NA_ROWS - 2)
        pick_dr = (dr[..., None] == np.arange(2 * NA_ROWS - 1)).astype(np.float32)
        bias = jnp.einsum("krd,hdxc->hkxrc", pick_dr, toe, precision=exact)
        ok = row_ok[:, None, :, None] & col_ok[None, :, None, :]
        tabs.append(jnp.where(jnp.asarray(ok)[None], bias * LOG2E, NEG_INF).reshape(n_heads, NA_K_TILES * TK, TQ))
    tabs.append(jnp.full_like(tabs[0], NEG_INF))
    return jnp.stack(tabs)


def _na_attn(q, k, vt, bias, *, seq, n_batch):
    t, w = q.shape
    nq = seq // TQ
    n_lat_blocks = n_batch * nq
    nv = vt.shape[1]
    qmap = _q_block_map(nq, n_batch)

    def kblock(tile):
        return lambda b, j: (b * nq + jnp.clip(j - 1, 0, nq - NA_K_TILES) + tile, 0)

    def vblock(tile):
        return lambda b, j: (b * nq + jnp.clip(j - 1, 0, nq - NA_K_TILES) + tile, 0, 0)

    variant = lambda b, j: (jnp.where(j == 0, 0, jnp.where(j == nq - 1, 2, jnp.where(j == nq, 3, 1))),
                            0, 0, 0)
    in_specs = ([pl.BlockSpec((TQ, w), qmap),
                 pl.BlockSpec((TK, w), lambda b, j: (n_lat_blocks + b, 0))]
                + [pl.BlockSpec((TK, w), kblock(i)) for i in range(NA_K_TILES)]
                + [pl.BlockSpec((1, nv, TK), lambda b, j: (n_lat_blocks + b, 0, 0))]
                + [pl.BlockSpec((1, nv, TK), vblock(i)) for i in range(NA_K_TILES)]
                + [pl.BlockSpec((1,) + bias.shape[1:], variant)])
    return pl.pallas_call(
        _na_attn_kernel,
        grid=(n_batch, nq + 1),
        in_specs=in_specs,
        out_specs=pl.BlockSpec((TQ, BRANCH_W), qmap),
        out_shape=jax.ShapeDtypeStruct((t, BRANCH_W), BF16),
        scratch_shapes=[pltpu.VMEM((BRANCH_W, TQ), F32)],
        compiler_params=_cparams(2),
        name="na_attn",
    )(q, k, k, k, k, vt, vt, vt, vt, bias)


def _gqa_attn_kernel(sink_ref, q_ref, kc_ref, k0_ref, k1_ref, k2_ref, vc_ref, v0_ref, v1_ref, v2_ref,
                     o_ref, ot_scr, *, nq):
    j = pl.program_id(1)
    k_refs = (kc_ref, k0_ref, k1_ref, k2_ref)
    v_refs = (vc_ref, v0_ref, v1_ref, v2_ref)
    key_i = lax.broadcasted_iota(jnp.int32, (TK, TQ), 0)
    qry_i = lax.broadcasted_iota(jnp.int32, (TK, TQ), 1)
    keeps = []
    for rel in (-1, 0, 1):
        tile = j + rel
        tile_ok = (tile >= 0) & (tile < nq) & (j < nq)
        reach = jnp.where(tile_ok, WINDOW, -1)
        keeps.append(jnp.abs(rel * TK + key_i - qry_i) <= reach)
    heads_per_kv = N_HEADS // GQA_KV_HEADS
    keys = [[r[:, _lanes(kv)] for r in k_refs] for kv in range(GQA_KV_HEADS)]
    scores = {}

    def produce(hd):
        kv = hd // heads_per_kv
        qm = _lane_masked(q_ref[:, _lanes(kv)], GQA_HEAD * (hd % heads_per_kv), GQA_HEAD)
        sc = [_nt_dot(kt, qm) for kt in keys[kv]]
        scores[hd] = [sc[0]] + [jnp.where(keeps[t], sc[1 + t], NEG_INF) for t in range(3)]

    def consume(hd):
        sink = jnp.full((1, TQ), sink_ref[hd] * LOG2E, F32)
        _, sink_weight, probs = _softmax_tiles(scores[hd], sink)
        acc = _pv([r[0, _vrows(hd // heads_per_kv), :] for r in v_refs], probs)
        ot_scr[hd * HEAD_V:(hd + 1) * HEAD_V, :] = _normalised(acc, sink_weight)

    _interleave(N_HEADS, produce, consume, N_HEADS)
    o_ref[...] = ot_scr[...].T.astype(BF16)


def _gqa_attn(q, k, vt, sink, *, seq, n_batch):
    t, w = q.shape
    nq = seq // TQ
    n_lat_blocks = n_batch * nq
    nv = vt.shape[1]
    qmap = _q_block_map(nq, n_batch)

    def kblock(rel):
        return lambda b, j: (b * nq + jnp.clip(j + rel, 0, nq - 1), 0)

    def vblock(rel):
        return lambda b, j: (b * nq + jnp.clip(j + rel, 0, nq - 1), 0, 0)

    in_specs = ([pl.BlockSpec(memory_space=pltpu.SMEM),
                 pl.BlockSpec((TQ, w), qmap),
                 pl.BlockSpec((TK, w), lambda b, j: (n_lat_blocks + b, 0))]
                + [pl.BlockSpec((TK, w), kblock(rel)) for rel in (-1, 0, 1)]
                + [pl.BlockSpec((1, nv, TK), lambda b, j: (n_lat_blocks + b, 0, 0))]
                + [pl.BlockSpec((1, nv, TK), vblock(rel)) for rel in (-1, 0, 1)])
    return pl.pallas_call(
        functools.partial(_gqa_attn_kernel, nq=nq),
        grid=(n_batch, nq + 1),
        in_specs=in_specs,
        out_specs=pl.BlockSpec((TQ, BRANCH_W), qmap),
        out_shape=jax.ShapeDtypeStruct((t, BRANCH_W), BF16),
        scratch_shapes=[pltpu.VMEM((BRANCH_W, TQ), F32)],
        compiler_params=_cparams(2),
        name="gqa_attn",
    )(sink, q, k, k, k, k, vt, vt, vt, vt)


def _merge_kernel(h_ref, mod_ref, g_ref, ya_ref, yb_ref, yc_ref, yd_ref, wg_ref, bg_ref, wb_ref, wo_ref,
                  o_ref):
    h = h_ref[...]
    mod = mod_ref[0]
    u = _modulate(h, g_ref[...], mod[3:4], mod[4:5]).astype(BF16)
    merged = None
    for i, y_ref in enumerate((ya_ref, yb_ref, yc_ref, yd_ref)):
        gate = jax.nn.sigmoid(jnp.dot(u, wg_ref[i], preferred_element_type=F32) + bg_ref[i])
        term = gate * jnp.dot(y_ref[...], wb_ref[i], preferred_element_type=F32)
        merged = term if merged is None else merged + term
    y = jnp.dot(merged.astype(BF16), wo_ref[...], preferred_element_type=F32)
    o_ref[...] = h + mod[5:6] * y


def _merge(h, mod_l, g, ys, wg, bg, wb, wo, *, seq, n_batch):
    t, d = h.shape
    tok = lambda i: (i, 0)
    const2 = lambda i: (0, 0)
    const3 = lambda i: (0, 0, 0)
    in_specs = ([pl.BlockSpec((TM, d), tok),
                 pl.BlockSpec((1, N_MOD, d), lambda i: (jnp.minimum(i * TM // seq, n_batch), 0, 0)),
                 pl.BlockSpec((1, d), const2)]
                + [pl.BlockSpec((TM, BRANCH_W), tok)] * 4
                + [pl.BlockSpec(wg.shape, const3), pl.BlockSpec(bg.shape, const3),
                   pl.BlockSpec(wb.shape, const3), pl.BlockSpec(wo.shape, const2)])
    return pl.pallas_call(
        _merge_kernel,
        grid=(t // TM,),
        in_specs=in_specs,
        out_specs=pl.BlockSpec((TM, d), tok),
        out_shape=jax.ShapeDtypeStruct((t, d), F32),
        compiler_params=_cparams(1),
        name="merge",
    )(h, mod_l, g, *ys, wg, bg, wb, wo)


def _rope_tables(seq, dim):
    tpos = jnp.arange(seq)
    rows = (tpos // GRID_W).astype(F32)
    cols = (tpos % GRID_W).astype(F32)
    half = dim // 2
    freqs = jnp.power(ROPE_BASE, -jnp.arange(0, half, 2, dtype=F32) / half)
    ang = jnp.concatenate([rows[:, None] * freqs] * 2 + [cols[:, None] * freqs] * 2, axis=1)
    nfreq = dim // 4
    sign = np.where((np.arange(dim) % (2 * nfreq)) < nfreq, -1.0, 1.0).astype(np.float32)
    reps = LANES // dim
    cos = jnp.tile(jnp.cos(ang), (1, reps))
    sin = jnp.tile(jnp.sin(ang) * sign, (1, reps))
    cos = jnp.concatenate([cos, jnp.ones((TM, LANES), F32)], axis=0)
    sin = jnp.concatenate([sin, jnp.zeros((TM, LANES), F32)], axis=0)
    return cos, sin


def _vt_weight(w_cols, n_heads):
    d_in = w_cols.shape[0]
    wt = w_cols.T.reshape(n_heads, HEAD_V, d_in)
    wt = jnp.pad(wt, ((0, 0), (0, VROWS - HEAD_V), (0, 0))).reshape(n_heads * VROWS, d_in)
    bias = np.zeros((n_heads, VROWS, 1), np.float32)
    bias[:, ONES_ROW] = 1.0
    return wt, jnp.asarray(bias.reshape(n_heads * VROWS, 1))


def _mixin_weights(w_in, q_norm, w_uq, kv_norm, w_ukv):
    offs = np.concatenate([[0], np.cumsum(IN_SIZES)])
    col = lambda i: w_in[:, offs[i]:offs[i + 1]]
    d = w_in.shape[0]
    kr = jnp.zeros((d, LANES), F32).at[:, MLA_NOPE:MLA_NOPE + MLA_ROPE].set(col(2))
    gk = col(10).reshape(d, GQA_KV_HEADS, 1, GQA_HEAD)
    gk2 = jnp.broadcast_to(gk, (d, GQA_KV_HEADS, 2, GQA_HEAD)).reshape(d, 2 * GQA_KV_HEADS * GQA_HEAD)
    wmain = jnp.concatenate([col(0), col(1), kr, col(3) * (DIFF_SCALE * LOG2E), col(4),
                             col(6) * (NA_SCALE * LOG2E), col(7), col(9) * (GQA_SCALE * LOG2E), gk2], axis=1)
    wt_b, bt_b = _vt_weight(col(5), N_HEADS)
    wt_c, bt_c = _vt_weight(col(8), N_HEADS)
    wt_d, bt_d = _vt_weight(col(11), GQA_KV_HEADS)
    qk = MLA_NOPE + MLA_ROPE
    wq2 = jnp.pad((w_uq * (MLA_SCALE * LOG2E)).reshape(MLA_Q_RANK, N_HEADS, qk),
                  ((0, 0), (0, 0), (0, LANES - qk))).reshape(MLA_Q_RANK, N_HEADS * LANES)
    ukv = w_ukv.reshape(MLA_KV_RANK, N_HEADS, MLA_NOPE + HEAD_V)
    wk2 = jnp.pad(ukv[:, :, :MLA_NOPE], ((0, 0), (0, 0), (0, LANES - MLA_NOPE))
                  ).reshape(MLA_KV_RANK, N_HEADS * LANES)
    wv2, bv2 = _vt_weight(ukv[:, :, MLA_NOPE:].reshape(MLA_KV_RANK, N_HEADS * HEAD_V), N_HEADS)
    return {"wmain": wmain.astype(BF16),
            "wt": jnp.concatenate([wt_b, wt_c, wt_d], axis=0).astype(BF16),
            "bt": jnp.concatenate([bt_b, bt_c, bt_d], axis=0),
            "qn": q_norm.reshape(1, -1), "kvn": kv_norm.reshape(1, -1),
            "wq2": wq2.astype(BF16), "wk2": wk2.astype(BF16), "wv2": wv2.astype(BF16), "bv2": bv2}


def _ffn_weights(w_gu, w_down):
    d, two_f = w_gu.shape
    d_ff = two_f // 2
    n_chunks = d_ff // FF_CHUNK
    gate = w_gu[:, :d_ff].reshape(d, n_chunks, FF_CHUNK)
    up = w_gu[:, d_ff:].reshape(d, n_chunks, FF_CHUNK)
    wgu_c = jnp.transpose(jnp.concatenate([gate, up], axis=2), (1, 0, 2)).astype(BF16)
    return wgu_c, w_down.astype(BF16)


def kernel(x, c, ctx, c_ctx, w_ada, b_ada, norm_ffn1, ffn1_w_gu, ffn1_w_down, norm_mix, w_in, mla_q_norm,
           mla_w_uq, mla_kv_norm, mla_w_ukv, diff_lam, diff_subln, na_rpb, gqa_sink, w_branch, w_gate,
           b_gate, w_out, norm_ffn2, ffn2_w_gu, ffn2_w_down, final_norm):
    n_batch, seq, d = x.shape
    ctx_len = ctx.shape[1]
    depth = w_ada.shape[0]
    rows = seq // GRID_W
    assert ctx_len == TK and seq % TM == 0 and (n_batch * ctx_len) % TM == 0
    assert rows % NA_Q_ROWS == 0 and rows >= NA_K_TILES * NA_Q_ROWS and seq // TQ >= 3
    n_lat = n_batch * seq
    dims = dict(seq=seq, n_batch=n_batch)

    r_pad = -(-(n_batch + 1) // 8) * 8
    cc = jnp.zeros((r_pad, d), F32).at[:n_batch].set(c).at[n_batch].set(c_ctx)
    mod = _modulation(cc, w_ada, b_ada).reshape(depth, r_pad, N_MOD, d)

    tabs = _rope_tables(seq, MLA_ROPE) + _rope_tables(seq, GQA_HEAD)
    h = (x.reshape(n_lat, d), ctx.reshape(n_batch * ctx_len, d))
    n_tok = n_lat + n_batch * ctx_len

    for l in range(depth):
        last = l == depth - 1
        lam_init = 0.8 - 0.6 * math.exp(-0.3 * l)
        mod_l = mod[l]
        row = lambda v: v.reshape(1, -1)
        wgu1, wdn1 = _ffn_weights(ffn1_w_gu[l], ffn1_w_down[l])
        wgu2, wdn2 = _ffn_weights(ffn2_w_gu[l], ffn2_w_down[l])

        h = _ffn(h, mod_l, row(norm_ffn1[l]), wgu1, wdn1, mod_base=0, n_tok_out=n_tok, **dims)

        mw = _mixin_weights(w_in[l], mla_q_norm[l], mla_w_uq[l], mla_kv_norm[l], mla_w_ukv[l])
        (qa, ka, vat, qb, kb, vbt, qc, kc, vct, qd, kd, vdt) = _mixin(
            h, mod_l, row(norm_mix[l]), tabs, mw, **dims)
        ya = _dense_attn(qa, ka, vat, diff=False, **dims)
        yb = _dense_attn(qb, kb, vbt, diff=True, lam_init=lam_init, lamp=diff_lam[l].astype(F32),
                         subln=diff_subln[l].reshape(HEAD_V, 1), **dims)
        yc = _na_attn(qc, kc, vct, _na_bias_tables(na_rpb[l], rows), **dims)
        yd = _gqa_attn(qd, kd, vdt, gqa_sink[l].astype(F32), **dims)
        h = _merge(h, mod_l, row(norm_mix[l]), (ya, yb, yc, yd), w_gate[l].astype(BF16),
                   b_gate[l].reshape(4, 1, d), w_branch[l].astype(BF16), w_out[l].astype(BF16), **dims)

        h = _ffn(h, mod_l, row(norm_ffn2[l]), wgu2, wdn2, mod_base=6,
                 n_tok_out=n_lat if last else n_tok, final_g=row(final_norm) if last else None, **dims)
    return h.reshape(n_batch, seq, d)
```

```python
import functools
import math

import numpy as np
import jax
import jax.numpy as jnp
from jax import lax
from jax.experimental import pallas as pl
from jax.experimental.pallas import tpu as pltpu

F32 = jnp.float32
BF16 = jnp.bfloat16

GRID_W = 64
N_MOD = 9
ROPE_BASE = 10000.0
NEG_INF = -1e30
LOG2E = math.log2(math.e)
EPS = 1e-6
SUBLN_EPS = 1e-5
N_HEADS = 4
HEAD_V = 64
MLA_Q_RANK = 256
MLA_KV_RANK = 128
MLA_NOPE = 64
MLA_ROPE = 32
MLA_SCALE = (MLA_NOPE + MLA_ROPE) ** -0.5
DIFF_HEAD = 32
DIFF_SCALE = DIFF_HEAD ** -0.5
NA_HEAD = 64
NA_ROWS = 8
NA_COLS = 16
NA_SCALE = NA_HEAD ** -0.5
GQA_KV_HEADS = 2
GQA_HEAD = 64
WINDOW = 128
GQA_SCALE = GQA_HEAD ** -0.5
BRANCH_W = 256
IN_SIZES = (256, 128, 32, 256, 256, 256, 256, 256, 256, 256, 128, 128)

LANES = 128
TQ = 256
TK = 256
MLA_SUB = 8
DIFF_SUB = 4
VROWS = 80
ONES_ROW = HEAD_V
TM_MERGE = 1024
TM = 512
FF_CHUNK = 256
VMEM_LIMIT = 56 * 1024 * 1024


def _cparams(n_axes):
    return pltpu.CompilerParams(dimension_semantics=("arbitrary",) * n_axes,
                                vmem_limit_bytes=VMEM_LIMIT)


def _rms(x, eps):
    return x * lax.rsqrt(jnp.mean(x * x, axis=-1, keepdims=True) + eps)


def _modulate(h, g, shift, scale):
    return (_rms(h, EPS) * g) * (1.0 + scale) + shift


def _nt_dot(a, b):
    return lax.dot_general(a, b, (((1,), (1,)), ((), ())), preferred_element_type=F32)


def _mod_kernel(s_ref, w_ref, b_ref, o_ref):
    s = s_ref[...]
    s = s * jax.nn.sigmoid(s)
    o_ref[0] = jnp.dot(s, w_ref[0], preferred_element_type=F32,
                       precision=lax.Precision.HIGHEST) + b_ref[0]


def _modulation(cc, w_ada, b_ada):
    depth, d, nd = w_ada.shape
    r = cc.shape[0]
    return pl.pallas_call(
        _mod_kernel,
        grid=(depth, nd // d),
        in_specs=[pl.BlockSpec((r, d), lambda l, n: (0, 0)),
                  pl.BlockSpec((1, d, d), lambda l, n: (l, 0, n)),
                  pl.BlockSpec((1, 1, d), lambda l, n: (l, 0, n))],
        out_specs=pl.BlockSpec((1, r, d), lambda l, n: (l, 0, n)),
        out_shape=jax.ShapeDtypeStruct((depth, r, nd), F32),
        compiler_params=_cparams(2),
        name="adaln_mod",
    )(cc, w_ada, b_ada.reshape(depth, 1, nd))


def _ffn_kernel(*refs, mod_base, final, n_lat_tiles):
    if n_lat_tiles is not None:
        x_ref, c_ref, *refs = refs
        h = jnp.where(pl.program_id(0) < n_lat_tiles, x_ref[...], c_ref[...])
    else:
        h_ref, *refs = refs
        h = h_ref[...]
    if final:
        mod_ref, g_ref, wgu_ref, wdn_ref, fn_ref, o_ref, a_scr = refs
    else:
        mod_ref, g_ref, wgu_ref, wdn_ref, o_ref, a_scr = refs
    mod = mod_ref[0]
    shift = mod[mod_base:mod_base + 1]
    scale = mod[mod_base + 1:mod_base + 2]
    gate = mod[mod_base + 2:mod_base + 3]
    u = _modulate(h, g_ref[...], shift, scale).astype(BF16)
    for j in range(wgu_ref.shape[0]):
        gu = jnp.dot(u, wgu_ref[j], preferred_element_type=F32)
        g = gu[:, :FF_CHUNK]
        up = gu[:, FF_CHUNK:]
        a_scr[:, j * FF_CHUNK:(j + 1) * FF_CHUNK] = (g * jax.nn.sigmoid(g) * up).astype(BF16)
    y = jnp.dot(a_scr[...], wdn_ref[...], preferred_element_type=F32)
    out = h + (0.5 * gate) * y
    if final:
        out = _rms(out, EPS) * fn_ref[...]
    o_ref[...] = out


def _ffn(h, mod_l, g, wgu_c, wdn, *, mod_base, n_tok_out, seq, n_batch, final_g=None):
    n_chunks, d = wgu_c.shape[:2]
    d_ff = wdn.shape[0]
    final = final_g is not None
    const2 = lambda i: (0, 0)
    if isinstance(h, tuple):
        n_lat_tiles = h[0].shape[0] // TM
        tokens = list(h)
        in_specs = [pl.BlockSpec((TM, d), lambda i: (jnp.minimum(i, n_lat_tiles - 1), 0)),
                    pl.BlockSpec((TM, d), lambda i: (jnp.maximum(i - n_lat_tiles, 0), 0))]
    else:
        n_lat_tiles = None
        tokens = [h]
        in_specs = [pl.BlockSpec((TM, d), lambda i: (i, 0))]
    in_specs += [pl.BlockSpec((1, N_MOD, d), lambda i: (jnp.minimum(i * TM // seq, n_batch), 0, 0)),
                 pl.BlockSpec((1, d), const2),
                 pl.BlockSpec((n_chunks, d, 2 * FF_CHUNK), lambda i: (0, 0, 0)),
                 pl.BlockSpec((d_ff, d), const2)]
    args = tokens + [mod_l, g, wgu_c, wdn]
    if final:
        in_specs.append(pl.BlockSpec((1, d), const2))
        args.append(final_g)
    return pl.pallas_call(
        functools.partial(_ffn_kernel, mod_base=mod_base, final=final, n_lat_tiles=n_lat_tiles),
        grid=(n_tok_out // TM,),
        in_specs=in_specs,
        out_specs=pl.BlockSpec((TM, d), lambda i: (i, 0)),
        out_shape=jax.ShapeDtypeStruct((n_tok_out, d), F32),
        scratch_shapes=[pltpu.VMEM((TM, d_ff), BF16)],
        compiler_params=_cparams(1),
        name="ffn",
    )(*args)


def _rope(x, cos, sgn_sin, half):
    lane = lax.broadcasted_iota(jnp.int32, (1, LANES), 1)
    first = (lane & (2 * half - 1)) < half
    outs = []
    for g in range(x.shape[1] // LANES):
        xg = x[:, g * LANES:(g + 1) * LANES]
        partner = jnp.where(first, pltpu.roll(xg, LANES - half, 1), pltpu.roll(xg, half, 1))
        outs.append(xg * cos + partner * sgn_sin)
    return outs


def _mixin_kernel(h_ref, mod_ref, g_ref, c32_ref, s32_ref, c64_ref, s64_ref,
                  wmain_ref, wt_ref, bt_ref, qn_ref, kvn_ref, wq2_ref, wk2_ref, wv2_ref, bv2_ref,
                  qa_ref, ka_ref, vat_ref, qb_ref, kb_ref, vbt_ref, qc_ref, kc_ref, vct_ref,
                  qd_ref, kd_ref, vdt_ref):
    mod = mod_ref[0]
    u = _modulate(h_ref[...], g_ref[...], mod[3:4], mod[4:5]).astype(BF16)
    pm = jnp.dot(u, wmain_ref[...], preferred_element_type=F32)
    c32, s32, c64, s64 = c32_ref[...], s32_ref[...], c64_ref[...], s64_ref[...]
    lane = lax.broadcasted_iota(jnp.int32, (1, LANES), 1)
    rope_lanes = (lane >= MLA_NOPE) & (lane < MLA_NOPE + MLA_ROPE)
    c_mla = jnp.where(rope_lanes, c32, 1.0)
    s_mla = jnp.where(rope_lanes, s32, 0.0)

    cqn = (_rms(pm[:, 0:256], EPS) * qn_ref[...]).astype(BF16)
    ckvn = (_rms(pm[:, 256:384], EPS) * kvn_ref[...]).astype(BF16)
    qa = _rope(jnp.dot(cqn, wq2_ref[...], preferred_element_type=F32), c_mla, s_mla, MLA_ROPE // 4)
    k_rope = _rope(pm[:, 384:512], c_mla, s_mla, MLA_ROPE // 4)[0]
    ka = jnp.dot(ckvn, wk2_ref[...], preferred_element_type=F32)
    for hd in range(N_HEADS):
        sl = slice(hd * LANES, (hd + 1) * LANES)
        qa_ref[:, sl] = qa[hd].astype(BF16)
        ka_ref[:, sl] = (ka[:, sl] + k_rope).astype(BF16)
    vat = _nt_dot(wv2_ref[...], ckvn) + bv2_ref[...]

    for idx, (ref, lo, cos, sin, half) in enumerate((
            (qb_ref, 512, c32, s32, DIFF_HEAD // 4), (kb_ref, 768, c32, s32, DIFF_HEAD // 4),
            (qd_ref, 1536, c64, s64, GQA_HEAD // 4), (kd_ref, 1792, c64, s64, GQA_HEAD // 4))):
        parts = _rope(pm[:, lo:lo + 256], cos, sin, half)
        for g in range(2):
            ref[:, g * LANES:(g + 1) * LANES] = parts[g].astype(BF16)
    qc_ref[...] = pm[:, 1024:1280].astype(BF16)
    kc_ref[...] = pm[:, 1280:1536].astype(BF16)

    vt = _nt_dot(wt_ref[...], u) + bt_ref[...]
    nv = N_HEADS * VROWS
    for c in range(TM // TK):
        cs = slice(c * TK, (c + 1) * TK)
        vat_ref[c] = vat[:, cs].astype(BF16)
        vbt_ref[c] = vt[0:nv, cs].astype(BF16)
        vct_ref[c] = vt[nv:2 * nv, cs].astype(BF16)
        vdt_ref[c] = vt[2 * nv:, cs].astype(BF16)


def _mixin(h, mod_l, g, tabs, w, *, seq, n_batch):
    t, d = h.shape
    n_lat_tiles = n_batch * seq // TM
    tiles_per_seq = seq // TM
    tok = lambda i: (i, 0)
    const2 = lambda i: (0, 0)
    tab_map = lambda i: (jnp.where(i < n_lat_tiles, i % tiles_per_seq, tiles_per_seq), 0)
    vt_map = lambda i: (i, 0, 0)
    nv = N_HEADS * VROWS
    nvd = GQA_KV_HEADS * VROWS
    full = lambda a: pl.BlockSpec(a.shape, const2)
    in_specs = [pl.BlockSpec((TM, d), tok),
                pl.BlockSpec((1, N_MOD, d), lambda i: (jnp.minimum(i * TM // seq, n_batch), 0, 0)),
                pl.BlockSpec((1, d), const2)]
    in_specs += [pl.BlockSpec((TM, LANES), tab_map)] * 4
    wlist = [w["wmain"], w["wt"], w["bt"], w["qn"], w["kvn"], w["wq2"], w["wk2"], w["wv2"], w["bv2"]]
    in_specs += [full(a) for a in wlist]
    nat = lambda width: (pl.BlockSpec((TM, width), tok), jax.ShapeDtypeStruct((t, width), BF16))
    tr = lambda rows: (pl.BlockSpec((TM // TK, rows, TK), vt_map),
                       jax.ShapeDtypeStruct((t // TK, rows, TK), BF16))
    outs = [nat(512), nat(512), tr(nv), nat(256), nat(256), tr(nv), nat(256), nat(256), tr(nv),
            nat(256), nat(256), tr(nvd)]
    return pl.pallas_call(
        _mixin_kernel,
        grid=(t // TM,),
        in_specs=in_specs,
        out_specs=[o[0] for o in outs],
        out_shape=[o[1] for o in outs],
        compiler_params=_cparams(1),
        name="mixer_in",
    )(h, mod_l, g, *tabs, *wlist)


def _lane_masked(qg, lo, width):
    lane = lax.broadcasted_iota(jnp.int32, (1, LANES), 1)
    return jnp.where((lane >= lo) & (lane < lo + width), qg, jnp.zeros_like(qg))


def _transposed(q):
    return q.astype(F32).T.astype(BF16)


def _interleave(n_chains, produce, consume, lag):
    for idx in range(n_chains + lag):
        if idx < n_chains:
            produce(idx)
        if idx >= lag:
            consume(idx - lag)


def _softmax_tiles(scores, m_prev, tile_max=None):
    m_new = m_prev
    if tile_max is None:
        tile_max = [jnp.max(s, axis=0, keepdims=True) for s in scores]
    for mt in tile_max:
        m_new = mt if m_new is None else jnp.maximum(m_new, mt)
    probs = [jnp.exp2(s - m_new).astype(BF16) for s in scores]
    alpha = None if m_prev is None else jnp.exp2(m_prev - m_new)
    return m_new, alpha, probs


def _pv(v_tiles, probs):
    out = None
    for vt, p in zip(v_tiles, probs):
        d = jnp.dot(vt, p, preferred_element_type=F32)
        out = d if out is None else out + d
    return out


def _normalised(acc, extra_denominator=None):
    den = acc[ONES_ROW:ONES_ROW + 1]
    if extra_denominator is not None:
        den = den + extra_denominator
    return acc[0:HEAD_V] / den


def _lanes(grp):
    return slice(grp * LANES, (grp + 1) * LANES)


def _vrows(hd):
    return slice(hd * VROWS, (hd + 1) * VROWS)


def _q_block_map(nq, n_batch):
    return lambda b, j: (jnp.where(j < nq, b * nq + j, n_batch * nq + b), 0)


def _dense_attn_kernel(*refs, diff, nq, lam_init):
    if diff:
        lamp_ref, subln_ref, *refs = refs
        chains = [(hd // 2, hd) for hd in range(N_HEADS) for _ in range(2)]
    else:
        chains = [(hd, hd) for hd in range(N_HEADS)]
    (q_ref, qn_ref, kc_ref, kl_ref, vc_ref, vl_ref, o_ref,
     ot_scr, m_scr, acc_scr, sa_scr, sb_scr, ma_scr, mb_scr, qt_scr) = refs
    n_chains = len(chains)
    j = pl.program_id(1)
    slot = j & 1
    n_sub = sa_scr.shape[1]
    n_steps = nq // n_sub
    n_live = jnp.where(j < nq, n_steps, 0)
    buffers = ((sa_scr, ma_scr), (sb_scr, mb_scr))
    last = n_steps - 1

    def transpose_queries(src_ref, dst_slot):
        for c, (grp, hd) in enumerate(chains):
            qg = src_ref[:, _lanes(grp)]
            if diff:
                qg = _lane_masked(qg, 2 * DIFF_HEAD * (hd % 2) + DIFF_HEAD * (c % 2), DIFF_HEAD)
            qt_scr[dst_slot, c] = _transposed(qg)

    def scores(kt, q_slot, c):
        return jnp.dot(kt, qt_scr[q_slot, c], preferred_element_type=F32)

    def score_into(dst, dst_max, step, q_slot):
        k_cache = {}

        def produce(c):
            grp = chains[c][0]
            for i in range(n_sub):
                if (i, grp) not in k_cache:
                    off = pl.multiple_of((step * n_sub + i) * TK, TK)
                    k_cache[(i, grp)] = kl_ref[pl.ds(off, TK), _lanes(grp)]
                sc = scores(k_cache[(i, grp)], q_slot, c)
                dst[c, i] = sc
                dst_max[c, i] = jnp.max(sc, axis=0, keepdims=True)

        return produce

    def accumulate_from(chain_scores, chain_max, v_tile, n_tiles, first):
        v_cache = {}

        def consume(c):
            hd = chains[c][1]
            m_new, alpha, probs = _softmax_tiles(chain_scores(c), None if first else m_scr[c],
                                                 chain_max(c))
            for i in range(n_tiles):
                if (i, hd) not in v_cache:
                    v_cache[(i, hd)] = v_tile(i, hd)
            pv = _pv([v_cache[(i, hd)] for i in range(n_tiles)], probs)
            acc_scr[c] = pv if first else acc_scr[c] * alpha + pv
            m_scr[c] = m_new

        return consume

    @pl.when(j == 0)
    def _():
        transpose_queries(q_ref, 0)
        fill = score_into(sa_scr, ma_scr, 0, 0)
        for c in range(n_chains):
            fill(c)

    transpose_queries(qn_ref, 1 - slot)

    def ctx_scores_of(c):
        return [scores(kc_ref[:, _lanes(chains[c][0])], slot, c)]

    def ctx_accumulate(chain_scores):
        return accumulate_from(chain_scores, lambda c: None, lambda i, hd: vc_ref[0, _vrows(hd), :], 1, True)

    def finish_head(hd):
        if diff:
            dl = lamp_ref[...]
            lam = (jnp.exp(jnp.sum(dl[0:1] * dl[1:2], axis=1, keepdims=True))
                   - jnp.exp(jnp.sum(dl[2:3] * dl[3:4], axis=1, keepdims=True)) + lam_init)
            o = _normalised(acc_scr[2 * hd]) - lam * _normalised(acc_scr[2 * hd + 1])
            o = o * lax.rsqrt(jnp.mean(o * o, axis=0, keepdims=True) + SUBLN_EPS)
            o = o * subln_ref[...] * (1.0 - lam_init)
        else:
            o = _normalised(acc_scr[hd])
        ot_scr[hd * HEAD_V:(hd + 1) * HEAD_V, :] = o

    def finishing(consume):
        def consume_and_finish(c):
            consume(c)
            if c + 1 == n_chains or chains[c + 1][1] != chains[c][1]:
                finish_head(chains[c][1])

        return consume_and_finish

    @pl.when(j == nq)
    def _():
        ctx_scores = [ctx_scores_of(c) for c in range(n_chains)]
        ctx_consume = finishing(ctx_accumulate(lambda c: ctx_scores[c]))
        for c in range(n_chains):
            ctx_consume(c)

    def step(t, parity, ahead_step, ahead_slot, with_ctx=False, finish=False):
        (src, src_max), (dst, dst_max) = buffers[parity], buffers[1 - parity]
        consume = accumulate_from(lambda c: [src[c, i] for i in range(n_sub)],
                                  lambda c: [src_max[c, i] for i in range(n_sub)],
                                  lambda i, hd: vl_ref[t * n_sub + i, _vrows(hd), :],
                                  n_sub, False)
        produce = score_into(dst, dst_max, ahead_step, ahead_slot)
        if with_ctx:
            ctx_scores, look_ahead, lat_consume = {}, produce, consume
            ctx_consume = ctx_accumulate(lambda c: ctx_scores[c])

            def produce(c):
                ctx_scores[c] = ctx_scores_of(c)
                look_ahead(c)

            def consume(c):
                ctx_consume(c)
                lat_consume(c)

        _interleave(n_chains, produce, finishing(consume) if finish else consume, 1)

    def body(t, carry):
        @pl.when(t == 0)
        def _():
            step(t, 0, 1, slot, with_ctx=True)

        for parity in range(2):
            @pl.when(((t & 1) == parity) & (t != 0) & (t != last))
            def _():
                step(t, parity, t + 1, slot)

        @pl.when(t == last)
        def _():
            step(t, 1, 0, 1 - slot, finish=True)

        return carry

    lax.fori_loop(0, n_live, body, 0)
    o_ref[...] = ot_scr[...].T.astype(BF16)


def _dense_attn(q, k, vt, *, seq, n_batch, diff, lam_init=0.0, lamp=None, subln=None):
    t, qw = q.shape
    kw = k.shape[1]
    nq = seq // TQ
    n_lat_blocks = n_batch * nq
    nv = vt.shape[1]
    n_chains = 2 * N_HEADS if diff else N_HEADS
    DENSE_SUB = DIFF_SUB if diff else MLA_SUB
    assert nq % (2 * DENSE_SUB) == 0 and nq // DENSE_SUB >= 4
    qmap = _q_block_map(nq, n_batch)
    in_specs = [pl.BlockSpec((TQ, qw), qmap),
                pl.BlockSpec((TQ, qw), lambda b, j: qmap(b, jnp.minimum(j + 1, nq))),
                pl.BlockSpec((TK, kw), lambda b, j: (n_lat_blocks + b, 0)),
                pl.BlockSpec((seq, kw), lambda b, j: (b, 0)),
                pl.BlockSpec((1, nv, TK), lambda b, j: (n_lat_blocks + b, 0, 0)),
                pl.BlockSpec((nq, nv, TK), lambda b, j: (b, 0, 0))]
    args = [q, q, k, k, vt, vt]
    if diff:
        in_specs = [pl.BlockSpec(lamp.shape, lambda b, j: (0, 0)),
                    pl.BlockSpec(subln.shape, lambda b, j: (0, 0))] + in_specs
        args = [lamp, subln] + args
    return pl.pallas_call(
        functools.partial(_dense_attn_kernel, diff=diff, nq=nq, lam_init=lam_init),
        grid=(n_batch, nq + 1),
        in_specs=in_specs,
        out_specs=pl.BlockSpec((TQ, BRANCH_W), qmap),
        out_shape=jax.ShapeDtypeStruct((t, BRANCH_W), BF16),
        scratch_shapes=([pltpu.VMEM((BRANCH_W, TQ), F32),
                         pltpu.VMEM((n_chains, 1, TQ), F32),
                         pltpu.VMEM((n_chains, VROWS, TQ), F32),
                         pltpu.VMEM((n_chains, DENSE_SUB, TK, TQ), F32),
                         pltpu.VMEM((n_chains, DENSE_SUB, TK, TQ), F32),
                         pltpu.VMEM((n_chains, DENSE_SUB, 1, TQ), F32),
                         pltpu.VMEM((n_chains, DENSE_SUB, 1, TQ), F32),
                         pltpu.VMEM((2, n_chains, LANES, TQ), BF16)]),
        compiler_params=_cparams(2),
        name="diff_attn" if diff else "mla_attn",
    )(*args)


NA_Q_ROWS = TQ // GRID_W
NA_K_TILES = 3


def _na_attn_kernel(q_ref, kc_ref, k0_ref, k1_ref, k2_ref, vc_ref, v0_ref, v1_ref, v2_ref, bias_ref,
                    o_ref, ot_scr):
    k_refs = (kc_ref, k0_ref, k1_ref, k2_ref)
    v_refs = (vc_ref, v0_ref, v1_ref, v2_ref)
    keys = [[r[:, _lanes(grp)] for r in k_refs] for grp in range(N_HEADS // 2)]
    scores = {}

    def produce(hd):
        grp = hd // 2
        qm = _lane_masked(q_ref[:, _lanes(grp)], NA_HEAD * (hd % 2), NA_HEAD)
        sc = [_nt_dot(kt, qm) for kt in keys[grp]]
        scores[hd] = [sc[0]] + [sc[1 + t] + bias_ref[0, hd, t * TK:(t + 1) * TK, :]
                                for t in range(NA_K_TILES)]

    def consume(hd):
        _, _, probs = _softmax_tiles(scores[hd], None)
        acc = _pv([r[0, _vrows(hd), :] for r in v_refs], probs)
        ot_scr[hd * HEAD_V:(hd + 1) * HEAD_V, :] = _normalised(acc)

    _interleave(N_HEADS, produce, consume, N_HEADS)
    o_ref[...] = ot_scr[...].T.astype(BF16)


def _na_bias_tables(rpb, rows):
    n_heads = rpb.shape[0]
    exact = lax.Precision.HIGHEST
    kc = np.arange(GRID_W)[:, None]
    c = np.arange(GRID_W)[None, :]
    cs = np.clip(c - NA_COLS // 2, 0, GRID_W - NA_COLS)
    col_ok = (kc >= cs) & (kc < cs + NA_COLS)
    dc = np.clip(kc - c + NA_COLS - 1, 0, 2 * NA_COLS - 2)
    pick_dc = (dc[..., None] == np.arange(2 * NA_COLS - 1)).astype(np.float32)
    toe = jnp.einsum("hde,xce->hdxc", rpb.astype(F32), pick_dc, precision=exact)
    tabs = []
    for r0 in (0, NA_Q_ROWS, rows - NA_Q_ROWS):
        base = int(np.clip(r0 - NA_ROWS // 2, 0, rows - NA_K_TILES * NA_Q_ROWS))
        kr = (base + np.arange(NA_K_TILES * NA_Q_ROWS))[:, None]
        r = (r0 + np.arange(NA_Q_ROWS))[None, :]
        rs = np.clip(r - NA_ROWS // 2, 0, rows - NA_ROWS)
        row_ok = (kr >= rs) & (kr < rs + NA_ROWS)
        dr = np.clip(kr - r + NA_ROWS - 1, 0, 2 * NA_ROWS - 2)
        pick_dr = (dr[..., None] == np.arange(2 * NA_ROWS - 1)).astype(np.float32)
        bias = jnp.einsum("krd,hdxc->hkxrc", pick_dr, toe, precision=exact)
        ok = row_ok[:, None, :, None] & col_ok[None, :, None, :]
        tabs.append(jnp.where(jnp.asarray(ok)[None], bias * LOG2E, NEG_INF).reshape(n_heads, NA_K_TILES * TK, TQ))
    tabs.append(jnp.full_like(tabs[0], NEG_INF))
    return jnp.stack(tabs)


def _na_attn(q, k, vt, bias, *, seq, n_batch):
    t, w = q.shape
    nq = seq // TQ
    n_lat_blocks = n_batch * nq
    nv = vt.shape[1]
    qmap = _q_block_map(nq, n_batch)

    def kblock(tile):
        return lambda b, j: (b * nq + jnp.clip(j - 1, 0, nq - NA_K_TILES) + tile, 0)

    def vblock(tile):
        return lambda b, j: (b * nq + jnp.clip(j - 1, 0, nq - NA_K_TILES) + tile, 0, 0)

    variant = lambda b, j: (jnp.where(j == 0, 0, jnp.where(j == nq - 1, 2, jnp.where(j == nq, 3, 1))),
                            0, 0, 0)
    in_specs = ([pl.BlockSpec((TQ, w), qmap),
                 pl.BlockSpec((TK, w), lambda b, j: (n_lat_blocks + b, 0))]
                + [pl.BlockSpec((TK, w), kblock(i)) for i in range(NA_K_TILES)]
                + [pl.BlockSpec((1, nv, TK), lambda b, j: (n_lat_blocks + b, 0, 0))]
                + [pl.BlockSpec((1, nv, TK), vblock(i)) for i in range(NA_K_TILES)]
                + [pl.BlockSpec((1,) + bias.shape[1:], variant)])
    return pl.pallas_call(
        _na_attn_kernel,
        grid=(n_batch, nq + 1),
        in_specs=in_specs,
        out_specs=pl.BlockSpec((TQ, BRANCH_W), qmap),
        out_shape=jax.ShapeDtypeStruct((t, BRANCH_W), BF16),
        scratch_shapes=[pltpu.VMEM((BRANCH_W, TQ), F32)],
        compiler_params=_cparams(2),
        name="na_attn",
    )(q, k, k, k, k, vt, vt, vt, vt, bias)


GQA_KBLK = WINDOW


def _gqa_attn_kernel(sink_ref, q_ref, kc_ref, k0_ref, k1_ref, k2_ref, k3_ref,
                     vc_ref, v0_ref, v1_ref, v2_ref, v3_ref, o_ref, ot_scr, *, nq):
    j = pl.program_id(1)
    k_refs = (kc_ref, k0_ref, k1_ref, k2_ref, k3_ref)
    v_refs = (vc_ref, v0_ref, v1_ref, v2_ref, v3_ref)
    key_i = lax.broadcasted_iota(jnp.int32, (GQA_KBLK, TQ), 0)
    qry_i = lax.broadcasted_iota(jnp.int32, (GQA_KBLK, TQ), 1)
    keeps = []
    for i in range(4):
        blk = 2 * j - 1 + i
        blk_ok = (blk >= 0) & (blk < 2 * nq) & (j < nq)
        reach = jnp.where(blk_ok, WINDOW, -1)
        keeps.append(jnp.abs((i - 1) * GQA_KBLK + key_i - qry_i) <= reach)
    heads_per_kv = N_HEADS // GQA_KV_HEADS
    keys = [[r[:, _lanes(kv)] for r in k_refs] for kv in range(GQA_KV_HEADS)]
    scores = {}

    def produce(hd):
        kv = hd // heads_per_kv
        qm = _lane_masked(q_ref[:, _lanes(kv)], GQA_HEAD * (hd % heads_per_kv), GQA_HEAD)
        sc = [_nt_dot(kt, qm) for kt in keys[kv]]
        scores[hd] = [sc[0]] + [jnp.where(keeps[i], sc[1 + i], NEG_INF) for i in range(4)]

    def consume(hd):
        sink = jnp.full((1, TQ), sink_ref[hd] * LOG2E, F32)
        _, sink_weight, probs = _softmax_tiles(scores[hd], sink)
        acc = _pv([r[0, _vrows(hd // heads_per_kv), :] for r in v_refs], probs)
        ot_scr[hd * HEAD_V:(hd + 1) * HEAD_V, :] = _normalised(acc, sink_weight)

    _interleave(N_HEADS, produce, consume, N_HEADS)
    o_ref[...] = ot_scr[...].T.astype(BF16)


def _gqa_attn(q, k, vt, sink, *, seq, n_batch):
    t, w = q.shape
    nq = seq // TQ
    n_lat_blocks = n_batch * nq
    nv = vt.shape[1]
    qmap = _q_block_map(nq, n_batch)

    per_tile = TK // GQA_KBLK

    def blk(j, i):
        return jnp.clip(per_tile * j - 1 + i, 0, per_tile * nq - 1)

    def kblock(i):
        return lambda b, j: (b * nq * per_tile + blk(j, i), 0)

    def vblock(i):
        return lambda b, j: (b * nq + blk(j, i) // per_tile, 0, blk(j, i) % per_tile)

    in_specs = ([pl.BlockSpec(memory_space=pltpu.SMEM),
                 pl.BlockSpec((TQ, w), qmap),
                 pl.BlockSpec((TK, w), lambda b, j: (n_lat_blocks + b, 0))]
                + [pl.BlockSpec((GQA_KBLK, w), kblock(i)) for i in range(4)]
                + [pl.BlockSpec((1, nv, TK), lambda b, j: (n_lat_blocks + b, 0, 0))]
                + [pl.BlockSpec((1, nv, GQA_KBLK), vblock(i)) for i in range(4)])
    return pl.pallas_call(
        functools.partial(_gqa_attn_kernel, nq=nq),
        grid=(n_batch, nq + 1),
        in_specs=in_specs,
        out_specs=pl.BlockSpec((TQ, BRANCH_W), qmap),
        out_shape=jax.ShapeDtypeStruct((t, BRANCH_W), BF16),
        scratch_shapes=[pltpu.VMEM((BRANCH_W, TQ), F32)],
        compiler_params=_cparams(2),
        name="gqa_attn",
    )(sink, q, k, k, k, k, k, vt, vt, vt, vt, vt)


def _merge_kernel(h_ref, mod_ref, g_ref, ya_ref, yb_ref, yc_ref, yd_ref, wg_ref, bg_ref, wb_ref, wo_ref,
                  o_ref):
    h = h_ref[...]
    mod = mod_ref[0]
    u = _modulate(h, g_ref[...], mod[3:4], mod[4:5]).astype(BF16)
    merged = None
    for i, y_ref in enumerate((ya_ref, yb_ref, yc_ref, yd_ref)):
        gate = jax.nn.sigmoid(jnp.dot(u, wg_ref[i], preferred_element_type=F32) + bg_ref[i])
        term = gate * jnp.dot(y_ref[...], wb_ref[i], preferred_element_type=F32)
        merged = term if merged is None else merged + term
    y = jnp.dot(merged.astype(BF16), wo_ref[...], preferred_element_type=F32)
    o_ref[...] = h + mod[5:6] * y


def _merge(h, mod_l, g, ys, wg, bg, wb, wo, *, seq, n_batch):
    t, d = h.shape
    tok = lambda i: (i, 0)
    const2 = lambda i: (0, 0)
    const3 = lambda i: (0, 0, 0)
    in_specs = ([pl.BlockSpec((TM_MERGE, d), tok),
                 pl.BlockSpec((1, N_MOD, d), lambda i: (jnp.minimum(i * TM_MERGE // seq, n_batch), 0, 0)),
                 pl.BlockSpec((1, d), const2)]
                + [pl.BlockSpec((TM_MERGE, BRANCH_W), tok)] * 4
                + [pl.BlockSpec(wg.shape, const3), pl.BlockSpec(bg.shape, const3),
                   pl.BlockSpec(wb.shape, const3), pl.BlockSpec(wo.shape, const2)])
    return pl.pallas_call(
        _merge_kernel,
        grid=(t // TM_MERGE,),
        in_specs=in_specs,
        out_specs=pl.BlockSpec((TM_MERGE, d), tok),
        out_shape=jax.ShapeDtypeStruct((t, d), F32),
        compiler_params=_cparams(1),
        name="merge",
    )(h, mod_l, g, *ys, wg, bg, wb, wo)


def _rope_tables(seq, dim):
    tpos = jnp.arange(seq)
    rows = (tpos // GRID_W).astype(F32)
    cols = (tpos % GRID_W).astype(F32)
    half = dim // 2
    freqs = jnp.power(ROPE_BASE, -jnp.arange(0, half, 2, dtype=F32) / half)
    ang = jnp.concatenate([rows[:, None] * freqs] * 2 + [cols[:, None] * freqs] * 2, axis=1)
    nfreq = dim // 4
    sign = np.where((np.arange(dim) % (2 * nfreq)) < nfreq, -1.0, 1.0).astype(np.float32)
    reps = LANES // dim
    cos = jnp.tile(jnp.cos(ang), (1, reps))
    sin = jnp.tile(jnp.sin(ang) * sign, (1, reps))
    cos = jnp.concatenate([cos, jnp.ones((TM, LANES), F32)], axis=0)
    sin = jnp.concatenate([sin, jnp.zeros((TM, LANES), F32)], axis=0)
    return cos, sin


def _vt_weight(w_cols, n_heads):
    d_in = w_cols.shape[0]
    wt = w_cols.T.reshape(n_heads, HEAD_V, d_in)
    wt = jnp.pad(wt, ((0, 0), (0, VROWS - HEAD_V), (0, 0))).reshape(n_heads * VROWS, d_in)
    bias = np.zeros((n_heads, VROWS, 1), np.float32)
    bias[:, ONES_ROW] = 1.0
    return wt, jnp.asarray(bias.reshape(n_heads * VROWS, 1))


def _mixin_weights(w_in, q_norm, w_uq, kv_norm, w_ukv):
    offs = np.concatenate([[0], np.cumsum(IN_SIZES)])
    col = lambda i: w_in[:, offs[i]:offs[i + 1]]
    d = w_in.shape[0]
    kr = jnp.zeros((d, LANES), F32).at[:, MLA_NOPE:MLA_NOPE + MLA_ROPE].set(col(2))
    gk = col(10).reshape(d, GQA_KV_HEADS, 1, GQA_HEAD)
    gk2 = jnp.broadcast_to(gk, (d, GQA_KV_HEADS, 2, GQA_HEAD)).reshape(d, 2 * GQA_KV_HEADS * GQA_HEAD)
    wmain = jnp.concatenate([col(0), col(1), kr, col(3) * (DIFF_SCALE * LOG2E), col(4),
                             col(6) * (NA_SCALE * LOG2E), col(7), col(9) * (GQA_SCALE * LOG2E), gk2], axis=1)
    wt_b, bt_b = _vt_weight(col(5), N_HEADS)
    wt_c, bt_c = _vt_weight(col(8), N_HEADS)
    wt_d, bt_d = _vt_weight(col(11), GQA_KV_HEADS)
    qk = MLA_NOPE + MLA_ROPE
    wq2 = jnp.pad((w_uq * (MLA_SCALE * LOG2E)).reshape(MLA_Q_RANK, N_HEADS, qk),
                  ((0, 0), (0, 0), (0, LANES - qk))).reshape(MLA_Q_RANK, N_HEADS * LANES)
    ukv = w_ukv.reshape(MLA_KV_RANK, N_HEADS, MLA_NOPE + HEAD_V)
    wk2 = jnp.pad(ukv[:, :, :MLA_NOPE], ((0, 0), (0, 0), (0, LANES - MLA_NOPE))
                  ).reshape(MLA_KV_RANK, N_HEADS * LANES)
    wv2, bv2 = _vt_weight(ukv[:, :, MLA_NOPE:].reshape(MLA_KV_RANK, N_HEADS * HEAD_V), N_HEADS)
    return {"wmain": wmain.astype(BF16),
            "wt": jnp.concatenate([wt_b, wt_c, wt_d], axis=0).astype(BF16),
            "bt": jnp.concatenate([bt_b, bt_c, bt_d], axis=0),
            "qn": q_norm.reshape(1, -1), "kvn": kv_norm.reshape(1, -1),
            "wq2": wq2.astype(BF16), "wk2": wk2.astype(BF16), "wv2": wv2.astype(BF16), "bv2": bv2}


def _ffn_weights(w_gu, w_down):
    d, two_f = w_gu.shape
    d_ff = two_f // 2
    n_chunks = d_ff // FF_CHUNK
    gate = w_gu[:, :d_ff].reshape(d, n_chunks, FF_CHUNK)
    up = w_gu[:, d_ff:].reshape(d, n_chunks, FF_CHUNK)
    wgu_c = jnp.transpose(jnp.concatenate([gate, up], axis=2), (1, 0, 2)).astype(BF16)
    return wgu_c, w_down.astype(BF16)


def kernel(x, c, ctx, c_ctx, w_ada, b_ada, norm_ffn1, ffn1_w_gu, ffn1_w_down, norm_mix, w_in, mla_q_norm,
           mla_w_uq, mla_kv_norm, mla_w_ukv, diff_lam, diff_subln, na_rpb, gqa_sink, w_branch, w_gate,
           b_gate, w_out, norm_ffn2, ffn2_w_gu, ffn2_w_down, final_norm):
    n_batch, seq, d = x.shape
    ctx_len = ctx.shape[1]
    depth = w_ada.shape[0]
    rows = seq // GRID_W
    assert ctx_len == TK and all(seq % tm == 0 and (n_batch * ctx_len) % tm == 0 for tm in (TM, TM_MERGE))
    assert rows % NA_Q_ROWS == 0 and rows >= NA_K_TILES * NA_Q_ROWS and seq // TQ >= 3
    n_lat = n_batch * seq
    dims = dict(seq=seq, n_batch=n_batch)

    r_pad = -(-(n_batch + 1) // 8) * 8
    cc = jnp.zeros((r_pad, d), F32).at[:n_batch].set(c).at[n_batch].set(c_ctx)
    mod = _modulation(cc, w_ada, b_ada).reshape(depth, r_pad, N_MOD, d)

    tabs = _rope_tables(seq, MLA_ROPE) + _rope_tables(seq, GQA_HEAD)
    h = (x.reshape(n_lat, d), ctx.reshape(n_batch * ctx_len, d))
    n_tok = n_lat + n_batch * ctx_len

    for l in range(depth):
        last = l == depth - 1
        lam_init = 0.8 - 0.6 * math.exp(-0.3 * l)
        mod_l = mod[l]
        row = lambda v: v.reshape(1, -1)
        wgu1, wdn1 = _ffn_weights(ffn1_w_gu[l], ffn1_w_down[l])
        wgu2, wdn2 = _ffn_weights(ffn2_w_gu[l], ffn2_w_down[l])

        h = _ffn(h, mod_l, row(norm_ffn1[l]), wgu1, wdn1, mod_base=0, n_tok_out=n_tok, **dims)

        mw = _mixin_weights(w_in[l], mla_q_norm[l], mla_w_uq[l], mla_kv_norm[l], mla_w_ukv[l])
        (qa, ka, vat, qb, kb, vbt, qc, kc, vct, qd, kd, vdt) = _mixin(
            h, mod_l, row(norm_mix[l]), tabs, mw, **dims)
        ya = _dense_attn(qa, ka, vat, diff=False, **dims)
        yb = _dense_attn(qb, kb, vbt, diff=True, lam_init=lam_init, lamp=diff_lam[l].astype(F32),
                         subln=diff_subln[l].reshape(HEAD_V, 1), **dims)
        yc = _na_attn(qc, kc, vct, _na_bias_tables(na_rpb[l], rows), **dims)
        yd = _gqa_attn(qd, kd, vdt, gqa_sink[l].astype(F32), **dims)
        h = _merge(h, mod_l, row(norm_mix[l]), (ya, yb, yc, yd), w_gate[l].astype(BF16),
                   b_gate[l].reshape(4, 1, d), w_branch[l].astype(BF16), w_out[l].astype(BF16), **dims)

        h = _ffn(h, mod_l, row(norm_ffn2[l]), wgu2, wdn2, mod_base=6,
                 n_tok_out=n_lat if last else n_tok, final_g=row(final_norm) if last else None, **dims)
    return h.reshape(n_batch, seq, d)
```

```python
import functools
import math

import numpy as np
import jax
import jax.numpy as jnp
from jax import lax
from jax.experimental import pallas as pl
from jax.experimental.pallas import tpu as pltpu

F32 = jnp.float32
BF16 = jnp.bfloat16

GRID_W = 64
N_MOD = 9
ROPE_BASE = 10000.0
NEG_INF = -1e30
LOG2E = math.log2(math.e)
EPS = 1e-6
SUBLN_EPS = 1e-5
N_HEADS = 4
HEAD_V = 64
MLA_Q_RANK = 256
MLA_KV_RANK = 128
MLA_NOPE = 64
MLA_ROPE = 32
MLA_SCALE = (MLA_NOPE + MLA_ROPE) ** -0.5
DIFF_HEAD = 32
DIFF_SCALE = DIFF_HEAD ** -0.5
NA_HEAD = 64
NA_ROWS = 8
NA_COLS = 16
NA_SCALE = NA_HEAD ** -0.5
GQA_KV_HEADS = 2
GQA_HEAD = 64
WINDOW = 128
GQA_SCALE = GQA_HEAD ** -0.5
BRANCH_W = 256
IN_SIZES = (256, 128, 32, 256, 256, 256, 256, 256, 256, 256, 128, 128)

LANES = 128
TQ = 256
TK = 256
MLA_SUB = 8
DIFF_SUB = 4
VROWS = 80
ONES_ROW = HEAD_V
TM_MERGE = 1024
TM = 512
FF_CHUNK = 256
VMEM_LIMIT = 56 * 1024 * 1024


def _cparams(n_axes):
    return pltpu.CompilerParams(dimension_semantics=("arbitrary",) * n_axes,
                                vmem_limit_bytes=VMEM_LIMIT)


def _rms(x, eps):
    return x * lax.rsqrt(jnp.mean(x * x, axis=-1, keepdims=True) + eps)


def _modulate(h, g, shift, scale):
    return (_rms(h, EPS) * g) * (1.0 + scale) + shift


def _nt_dot(a, b):
    return lax.dot_general(a, b, (((1,), (1,)), ((), ())), preferred_element_type=F32)


def _mod_kernel(s_ref, w_ref, b_ref, o_ref):
    s = s_ref[...]
    s = s * jax.nn.sigmoid(s)
    o_ref[0] = jnp.dot(s, w_ref[0], preferred_element_type=F32,
                       precision=lax.Precision.HIGHEST) + b_ref[0]


def _modulation(cc, w_ada, b_ada):
    depth, d, nd = w_ada.shape
    r = cc.shape[0]
    return pl.pallas_call(
        _mod_kernel,
        grid=(depth, nd // d),
        in_specs=[pl.BlockSpec((r, d), lambda l, n: (0, 0)),
                  pl.BlockSpec((1, d, d), lambda l, n: (l, 0, n)),
                  pl.BlockSpec((1, 1, d), lambda l, n: (l, 0, n))],
        out_specs=pl.BlockSpec((1, r, d), lambda l, n: (l, 0, n)),
        out_shape=jax.ShapeDtypeStruct((depth, r, nd), F32),
        compiler_params=_cparams(2),
        name="adaln_mod",
    )(cc, w_ada, b_ada.reshape(depth, 1, nd))


def _ffn_kernel(*refs, mod_base, final, n_lat_tiles):
    if n_lat_tiles is not None:
        x_ref, c_ref, *refs = refs
        h = jnp.where(pl.program_id(0) < n_lat_tiles, x_ref[...], c_ref[...])
    else:
        h_ref, *refs = refs
        h = h_ref[...]
    if final:
        mod_ref, g_ref, wgu_ref, wdn_ref, fn_ref, o_ref, a_scr = refs
    else:
        mod_ref, g_ref, wgu_ref, wdn_ref, o_ref, a_scr = refs
    mod = mod_ref[0]
    shift = mod[mod_base:mod_base + 1]
    scale = mod[mod_base + 1:mod_base + 2]
    gate = mod[mod_base + 2:mod_base + 3]
    u = _modulate(h, g_ref[...], shift, scale).astype(BF16)
    for j in range(wgu_ref.shape[0]):
        gu = jnp.dot(u, wgu_ref[j], preferred_element_type=F32)
        g = gu[:, :FF_CHUNK]
        up = gu[:, FF_CHUNK:]
        a_scr[:, j * FF_CHUNK:(j + 1) * FF_CHUNK] = (g * jax.nn.sigmoid(g) * up).astype(BF16)
    y = jnp.dot(a_scr[...], wdn_ref[...], preferred_element_type=F32)
    out = h + (0.5 * gate) * y
    if final:
        out = _rms(out, EPS) * fn_ref[...]
    o_ref[...] = out


def _ffn(h, mod_l, g, wgu_c, wdn, *, mod_base, n_tok_out, seq, n_batch, final_g=None):
    n_chunks, d = wgu_c.shape[:2]
    d_ff = wdn.shape[0]
    final = final_g is not None
    const2 = lambda i: (0, 0)
    if isinstance(h, tuple):
        n_lat_tiles = h[0].shape[0] // TM
        tokens = list(h)
        in_specs = [pl.BlockSpec((TM, d), lambda i: (jnp.minimum(i, n_lat_tiles - 1), 0)),
                    pl.BlockSpec((TM, d), lambda i: (jnp.maximum(i - n_lat_tiles, 0), 0))]
    else:
        n_lat_tiles = None
        tokens = [h]
        in_specs = [pl.BlockSpec((TM, d), lambda i: (i, 0))]
    in_specs += [pl.BlockSpec((1, N_MOD, d), lambda i: (jnp.minimum(i * TM // seq, n_batch), 0, 0)),
                 pl.BlockSpec((1, d), const2),
                 pl.BlockSpec((n_chunks, d, 2 * FF_CHUNK), lambda i: (0, 0, 0)),
                 pl.BlockSpec((d_ff, d), const2)]
    args = tokens + [mod_l, g, wgu_c, wdn]
    if final:
        in_specs.append(pl.BlockSpec((1, d), const2))
        args.append(final_g)
    return pl.pallas_call(
        functools.partial(_ffn_kernel, mod_base=mod_base, final=final, n_lat_tiles=n_lat_tiles),
        grid=(n_tok_out // TM,),
        in_specs=in_specs,
        out_specs=pl.BlockSpec((TM, d), lambda i: (i, 0)),
        out_shape=jax.ShapeDtypeStruct((n_tok_out, d), F32),
        scratch_shapes=[pltpu.VMEM((TM, d_ff), BF16)],
        compiler_params=_cparams(1),
        name="ffn",
    )(*args)


def _rope(x, cos, sgn_sin, half):
    lane = lax.broadcasted_iota(jnp.int32, (1, LANES), 1)
    first = (lane & (2 * half - 1)) < half
    outs = []
    for g in range(x.shape[1] // LANES):
        xg = x[:, g * LANES:(g + 1) * LANES]
        partner = jnp.where(first, pltpu.roll(xg, LANES - half, 1), pltpu.roll(xg, half, 1))
        outs.append(xg * cos + partner * sgn_sin)
    return outs


def _mixin_kernel(h_ref, mod_ref, g_ref, c32_ref, s32_ref, c64_ref, s64_ref,
                  wmain_ref, wt_ref, bt_ref, qn_ref, kvn_ref, wq2_ref, wk2_ref, wv2_ref, bv2_ref,
                  qa_ref, ka_ref, vat_ref, qb_ref, kb_ref, vbt_ref, qc_ref, kc_ref, vct_ref,
                  qd_ref, kd_ref, vdt_ref):
    mod = mod_ref[0]
    u = _modulate(h_ref[...], g_ref[...], mod[3:4], mod[4:5]).astype(BF16)
    pm = jnp.dot(u, wmain_ref[...], preferred_element_type=F32)
    c32, s32, c64, s64 = c32_ref[...], s32_ref[...], c64_ref[...], s64_ref[...]
    lane = lax.broadcasted_iota(jnp.int32, (1, LANES), 1)
    rope_lanes = (lane >= MLA_NOPE) & (lane < MLA_NOPE + MLA_ROPE)
    c_mla = jnp.where(rope_lanes, c32, 1.0)
    s_mla = jnp.where(rope_lanes, s32, 0.0)

    cqn = (_rms(pm[:, 0:256], EPS) * qn_ref[...]).astype(BF16)
    ckvn = (_rms(pm[:, 256:384], EPS) * kvn_ref[...]).astype(BF16)
    qa = _rope(jnp.dot(cqn, wq2_ref[...], preferred_element_type=F32), c_mla, s_mla, MLA_ROPE // 4)
    k_rope = _rope(pm[:, 384:512], c_mla, s_mla, MLA_ROPE // 4)[0]
    ka = jnp.dot(ckvn, wk2_ref[...], preferred_element_type=F32)
    for hd in range(N_HEADS):
        sl = slice(hd * LANES, (hd + 1) * LANES)
        qa_ref[:, sl] = qa[hd].astype(BF16)
        ka_ref[:, sl] = (ka[:, sl] + k_rope).astype(BF16)
    vat = _nt_dot(wv2_ref[...], ckvn) + bv2_ref[...]

    for idx, (ref, lo, cos, sin, half) in enumerate((
            (qb_ref, 512, c32, s32, DIFF_HEAD // 4), (kb_ref, 768, c32, s32, DIFF_HEAD // 4),
            (qd_ref, 1536, c64, s64, GQA_HEAD // 4), (kd_ref, 1792, c64, s64, GQA_HEAD // 4))):
        parts = _rope(pm[:, lo:lo + 256], cos, sin, half)
        for g in range(2):
            ref[:, g * LANES:(g + 1) * LANES] = parts[g].astype(BF16)
    qc_ref[...] = pm[:, 1024:1280].astype(BF16)
    kc_ref[...] = pm[:, 1280:1536].astype(BF16)

    vt = _nt_dot(wt_ref[...], u) + bt_ref[...]
    nv = N_HEADS * VROWS
    for c in range(TM // TK):
        cs = slice(c * TK, (c + 1) * TK)
        vat_ref[c] = vat[:, cs].astype(BF16)
        vbt_ref[c] = vt[0:nv, cs].astype(BF16)
        vct_ref[c] = vt[nv:2 * nv, cs].astype(BF16)
        vdt_ref[c] = vt[2 * nv:, cs].astype(BF16)


def _mixin(h, mod_l, g, tabs, w, *, seq, n_batch):
    t, d = h.shape
    n_lat_tiles = n_batch * seq // TM
    tiles_per_seq = seq // TM
    tok = lambda i: (i, 0)
    const2 = lambda i: (0, 0)
    tab_map = lambda i: (jnp.where(i < n_lat_tiles, i % tiles_per_seq, tiles_per_seq), 0)
    vt_map = lambda i: (i, 0, 0)
    nv = N_HEADS * VROWS
    nvd = GQA_KV_HEADS * VROWS
    full = lambda a: pl.BlockSpec(a.shape, const2)
    in_specs = [pl.BlockSpec((TM, d), tok),
                pl.BlockSpec((1, N_MOD, d), lambda i: (jnp.minimum(i * TM // seq, n_batch), 0, 0)),
                pl.BlockSpec((1, d), const2)]
    in_specs += [pl.BlockSpec((TM, LANES), tab_map)] * 4
    wlist = [w["wmain"], w["wt"], w["bt"], w["qn"], w["kvn"], w["wq2"], w["wk2"], w["wv2"], w["bv2"]]
    in_specs += [full(a) for a in wlist]
    nat = lambda width: (pl.BlockSpec((TM, width), tok), jax.ShapeDtypeStruct((t, width), BF16))
    tr = lambda rows: (pl.BlockSpec((TM // TK, rows, TK), vt_map),
                       jax.ShapeDtypeStruct((t // TK, rows, TK), BF16))
    outs = [nat(512), nat(512), tr(nv), nat(256), nat(256), tr(nv), nat(256), nat(256), tr(nv),
            nat(256), nat(256), tr(nvd)]
    return pl.pallas_call(
        _mixin_kernel,
        grid=(t // TM,),
        in_specs=in_specs,
        out_specs=[o[0] for o in outs],
        out_shape=[o[1] for o in outs],
        compiler_params=_cparams(1),
        name="mixer_in",
    )(h, mod_l, g, *tabs, *wlist)


def _lane_masked(qg, lo, width):
    lane = lax.broadcasted_iota(jnp.int32, (1, LANES), 1)
    return jnp.where((lane >= lo) & (lane < lo + width), qg, jnp.zeros_like(qg))


def _transposed(q):
    return q.astype(F32).T.astype(BF16)


def _interleave(n_chains, produce, consume, lag):
    for idx in range(n_chains + lag):
        if idx < n_chains:
            produce(idx)
        if idx >= lag:
            consume(idx - lag)


def _softmax_tiles(scores, m_prev, tile_max=None):
    m_new = m_prev
    if tile_max is None:
        tile_max = [jnp.max(s, axis=0, keepdims=True) for s in scores]
    for mt in tile_max:
        m_new = mt if m_new is None else jnp.maximum(m_new, mt)
    probs = [jnp.exp2(s - m_new).astype(BF16) for s in scores]
    alpha = None if m_prev is None else jnp.exp2(m_prev - m_new)
    return m_new, alpha, probs


def _pv(v_tiles, probs):
    out = None
    for vt, p in zip(v_tiles, probs):
        d = jnp.dot(vt, p, preferred_element_type=F32)
        out = d if out is None else out + d
    return out


def _normalised(acc, extra_denominator=None):
    den = acc[ONES_ROW:ONES_ROW + 1]
    if extra_denominator is not None:
        den = den + extra_denominator
    return acc[0:HEAD_V] / den


def _lanes(grp):
    return slice(grp * LANES, (grp + 1) * LANES)


def _vrows(hd):
    return slice(hd * VROWS, (hd + 1) * VROWS)


def _q_block_map(nq, n_batch):
    return lambda b, j: (jnp.where(j < nq, b * nq + j, n_batch * nq + b), 0)


def _dense_attn_kernel(*refs, diff, nq, lam_init):
    if diff:
        lamp_ref, subln_ref, *refs = refs
        chains = [(hd // 2, hd) for hd in range(N_HEADS) for _ in range(2)]
    else:
        chains = [(hd, hd) for hd in range(N_HEADS) for _ in range(2)]
    (q_ref, qn_ref, kc_ref, kl_ref, vc_ref, vl_ref, o_ref,
     ot_scr, m_scr, acc_scr, sa_scr, sb_scr, ma_scr, mb_scr, qt_scr) = refs
    n_chains = len(chains)
    j = pl.program_id(1)
    slot = j & 1
    per_chain = sa_scr.shape[1]
    n_sub = per_chain if diff else 2 * per_chain
    first_tile = lambda c: 0 if diff else (c % 2) * per_chain
    q_index = lambda c: c if diff else c // 2
    n_steps = nq // n_sub
    n_live = jnp.where(j < nq, n_steps, 0)
    buffers = ((sa_scr, ma_scr), (sb_scr, mb_scr))
    last = n_steps - 1

    def transpose_queries(src_ref, dst_slot):
        for c, (grp, hd) in enumerate(chains):
            if not diff and c % 2:
                continue
            qg = src_ref[:, _lanes(grp)]
            if diff:
                qg = _lane_masked(qg, 2 * DIFF_HEAD * (hd % 2) + DIFF_HEAD * (c % 2), DIFF_HEAD)
            qt_scr[dst_slot, q_index(c)] = _transposed(qg)

    def scores(kt, q_slot, c):
        return jnp.dot(kt, qt_scr[q_slot, q_index(c)], preferred_element_type=F32)

    def score_into(dst, dst_max, step, q_slot):
        k_cache = {}

        def produce(c):
            grp = chains[c][0]
            for i in range(per_chain):
                g = first_tile(c) + i
                if (g, grp) not in k_cache:
                    off = pl.multiple_of((step * n_sub + g) * TK, TK)
                    k_cache[(g, grp)] = kl_ref[pl.ds(off, TK), _lanes(grp)]
                sc = scores(k_cache[(g, grp)], q_slot, c)
                dst[c, i] = sc
                dst_max[c, i] = jnp.max(sc, axis=0, keepdims=True)

        return produce

    def accumulate_from(chain_scores, chain_max, v_tile, n_tiles, first):
        v_cache = {}

        def consume(c):
            hd = chains[c][1]
            m_new, alpha, probs = _softmax_tiles(chain_scores(c), None if first else m_scr[c],
                                                 chain_max(c))
            vs = []
            for i in range(n_tiles):
                g = first_tile(c) + i
                if (g, hd) not in v_cache:
                    v_cache[(g, hd)] = v_tile(g, hd)
                vs.append(v_cache[(g, hd)])
            pv = _pv(vs, probs)
            acc_scr[c] = pv if first else acc_scr[c] * alpha + pv
            m_scr[c] = m_new

        return consume

    @pl.when(j == 0)
    def _():
        transpose_queries(q_ref, 0)
        fill = score_into(sa_scr, ma_scr, 0, 0)
        for c in range(n_chains):
            fill(c)

    transpose_queries(qn_ref, 1 - slot)

    def ctx_scores_of(c):
        if not diff and c % 2:
            return None
        return [scores(kc_ref[:, _lanes(chains[c][0])], slot, c)]

    def ctx_accumulate(chain_scores):
        real = accumulate_from(chain_scores, lambda c: None, lambda i, hd: vc_ref[0, _vrows(hd), :], 1, True)

        def consume(c):
            if diff or c % 2 == 0:
                real(c)
            else:
                acc_scr[c] = jnp.zeros((VROWS, TQ), F32)
                m_scr[c] = jnp.full((1, TQ), NEG_INF, F32)

        return consume

    def finish_head(hd):
        if diff:
            dl = lamp_ref[...]
            lam = (jnp.exp(jnp.sum(dl[0:1] * dl[1:2], axis=1, keepdims=True))
                   - jnp.exp(jnp.sum(dl[2:3] * dl[3:4], axis=1, keepdims=True)) + lam_init)
            o = _normalised(acc_scr[2 * hd]) - lam * _normalised(acc_scr[2 * hd + 1])
            o = o * lax.rsqrt(jnp.mean(o * o, axis=0, keepdims=True) + SUBLN_EPS)
            o = o * subln_ref[...] * (1.0 - lam_init)
        else:
            ma, mb = m_scr[2 * hd], m_scr[2 * hd + 1]
            m = jnp.maximum(ma, mb)
            o = _normalised(acc_scr[2 * hd] * jnp.exp2(ma - m) + acc_scr[2 * hd + 1] * jnp.exp2(mb - m))
        ot_scr[hd * HEAD_V:(hd + 1) * HEAD_V, :] = o

    def finishing(consume):
        def consume_and_finish(c):
            consume(c)
            if c + 1 == n_chains or chains[c + 1][1] != chains[c][1]:
                finish_head(chains[c][1])

        return consume_and_finish

    @pl.when(j == nq)
    def _():
        ctx_scores = [ctx_scores_of(c) for c in range(n_chains)]
        ctx_consume = finishing(ctx_accumulate(lambda c: ctx_scores[c]))
        for c in range(n_chains):
            ctx_consume(c)

    def step(t, parity, ahead_step, ahead_slot, with_ctx=False, finish=False):
        (src, src_max), (dst, dst_max) = buffers[parity], buffers[1 - parity]
        consume = accumulate_from(lambda c: [src[c, i] for i in range(per_chain)],
                                  lambda c: [src_max[c, i] for i in range(per_chain)],
                                  lambda g, hd: vl_ref[t * n_sub + g, _vrows(hd), :],
                                  per_chain, False)
        produce = score_into(dst, dst_max, ahead_step, ahead_slot)
        if with_ctx:
            ctx_scores, look_ahead, lat_consume = {}, produce, consume
            ctx_consume = ctx_accumulate(lambda c: ctx_scores[c])

            def produce(c):
                ctx_scores[c] = ctx_scores_of(c)
                look_ahead(c)

            def consume(c):
                ctx_consume(c)
                lat_consume(c)

        _interleave(n_chains, produce, finishing(consume) if finish else consume, 1)

    def body(t, carry):
        @pl.when(t == 0)
        def _():
            step(t, 0, 1, slot, with_ctx=True)

        for parity in range(2):
            @pl.when(((t & 1) == parity) & (t != 0) & (t != last))
            def _():
                step(t, parity, t + 1, slot)

        @pl.when(t == last)
        def _():
            step(t, 1, 0, 1 - slot, finish=True)

        return carry

    lax.fori_loop(0, n_live, body, 0)
    o_ref[...] = ot_scr[...].T.astype(BF16)


def _dense_attn(q, k, vt, *, seq, n_batch, diff, lam_init=0.0, lamp=None, subln=None):
    t, qw = q.shape
    kw = k.shape[1]
    nq = seq // TQ
    n_lat_blocks = n_batch * nq
    nv = vt.shape[1]
    n_chains = 2 * N_HEADS
    DENSE_SUB = DIFF_SUB if diff else MLA_SUB
    per_chain = DENSE_SUB if diff else DENSE_SUB // 2
    assert nq % (2 * DENSE_SUB) == 0 and nq // DENSE_SUB >= 4
    qmap = _q_block_map(nq, n_batch)
    in_specs = [pl.BlockSpec((TQ, qw), qmap),
                pl.BlockSpec((TQ, qw), lambda b, j: qmap(b, jnp.minimum(j + 1, nq))),
                pl.BlockSpec((TK, kw), lambda b, j: (n_lat_blocks + b, 0)),
                pl.BlockSpec((seq, kw), lambda b, j: (b, 0)),
                pl.BlockSpec((1, nv, TK), lambda b, j: (n_lat_blocks + b, 0, 0)),
                pl.BlockSpec((nq, nv, TK), lambda b, j: (b, 0, 0))]
    args = [q, q, k, k, vt, vt]
    if diff:
        in_specs = [pl.BlockSpec(lamp.shape, lambda b, j: (0, 0)),
                    pl.BlockSpec(subln.shape, lambda b, j: (0, 0))] + in_specs
        args = [lamp, subln] + args
    return pl.pallas_call(
        functools.partial(_dense_attn_kernel, diff=diff, nq=nq, lam_init=lam_init),
        grid=(n_batch, nq + 1),
        in_specs=in_specs,
        out_specs=pl.BlockSpec((TQ, BRANCH_W), qmap),
        out_shape=jax.ShapeDtypeStruct((t, BRANCH_W), BF16),
        scratch_shapes=([pltpu.VMEM((BRANCH_W, TQ), F32),
                         pltpu.VMEM((n_chains, 1, TQ), F32),
                         pltpu.VMEM((n_chains, VROWS, TQ), F32),
                         pltpu.VMEM((n_chains, per_chain, TK, TQ), F32),
                         pltpu.VMEM((n_chains, per_chain, TK, TQ), F32),
                         pltpu.VMEM((n_chains, per_chain, 1, TQ), F32),
                         pltpu.VMEM((n_chains, per_chain, 1, TQ), F32),
                         pltpu.VMEM((2, n_chains, LANES, TQ), BF16)]),
        compiler_params=_cparams(2),
        name="diff_attn" if diff else "mla_attn",
    )(*args)


NA_Q_ROWS = TQ // GRID_W
NA_K_TILES = 3


def _na_attn_kernel(q_ref, kc_ref, k0_ref, k1_ref, k2_ref, vc_ref, v0_ref, v1_ref, v2_ref, bias_ref,
                    o_ref, ot_scr):
    k_refs = (kc_ref, k0_ref, k1_ref, k2_ref)
    v_refs = (vc_ref, v0_ref, v1_ref, v2_ref)
    keys = [[r[:, _lanes(grp)] for r in k_refs] for grp in range(N_HEADS // 2)]
    scores = {}

    def produce(hd):
        grp = hd // 2
        qm = _lane_masked(q_ref[:, _lanes(grp)], NA_HEAD * (hd % 2), NA_HEAD)
        sc = [_nt_dot(kt, qm) for kt in keys[grp]]
        scores[hd] = [sc[0]] + [sc[1 + t] + bias_ref[0, hd, t * TK:(t + 1) * TK, :]
                                for t in range(NA_K_TILES)]

    def consume(hd):
        _, _, probs = _softmax_tiles(scores[hd], None)
        acc = _pv([r[0, _vrows(hd), :] for r in v_refs], probs)
        ot_scr[hd * HEAD_V:(hd + 1) * HEAD_V, :] = _normalised(acc)

    _interleave(N_HEADS, produce, consume, N_HEADS)
    o_ref[...] = ot_scr[...].T.astype(BF16)


def _na_bias_tables(rpb, rows):
    n_heads = rpb.shape[0]
    exact = lax.Precision.HIGHEST
    kc = np.arange(GRID_W)[:, None]
    c = np.arange(GRID_W)[None, :]
    cs = np.clip(c - NA_COLS // 2, 0, GRID_W - NA_COLS)
    col_ok = (kc >= cs) & (kc < cs + NA_COLS)
    dc = np.clip(kc - c + NA_COLS - 1, 0, 2 * NA_COLS - 2)
    pick_dc = (dc[..., None] == np.arange(2 * NA_COLS - 1)).astype(np.float32)
    toe = jnp.einsum("hde,xce->hdxc", rpb.astype(F32), pick_dc, precision=exact)
    tabs = []
    for r0 in (0, NA_Q_ROWS, rows - NA_Q_ROWS):
        base = int(np.clip(r0 - NA_ROWS // 2, 0, rows - NA_K_TILES * NA_Q_ROWS))
        kr = (base + np.arange(NA_K_TILES * NA_Q_ROWS))[:, None]
        r = (r0 + np.arange(NA_Q_ROWS))[None, :]
        rs = np.clip(r - NA_ROWS // 2, 0, rows - NA_ROWS)
        row_ok = (kr >= rs) & (kr < rs + NA_ROWS)
        dr = np.clip(kr - r + NA_ROWS - 1, 0, 2 * NA_ROWS - 2)
        pick_dr = (dr[..., None] == np.arange(2 * NA_ROWS - 1)).astype(np.float32)
        bias = jnp.einsum("krd,hdxc->hkxrc", pick_dr, toe, precision=exact)
        ok = row_ok[:, None, :, None] & col_ok[None, :, None, :]
        tabs.append(jnp.where(jnp.asarray(ok)[None], bias * LOG2E, NEG_INF).reshape(n_heads, NA_K_TILES * TK, TQ))
    tabs.append(jnp.full_like(tabs[0], NEG_INF))
    return jnp.stack(tabs)


def _na_attn(q, k, vt, bias, *, seq, n_batch):
    t, w = q.shape
    nq = seq // TQ
    n_lat_blocks = n_batch * nq
    nv = vt.shape[1]
    qmap = _q_block_map(nq, n_batch)

    def kblock(tile):
        return lambda b, j: (b * nq + jnp.clip(j - 1, 0, nq - NA_K_TILES) + tile, 0)

    def vblock(tile):
        return lambda b, j: (b * nq + jnp.clip(j - 1, 0, nq - NA_K_TILES) + tile, 0, 0)

    variant = lambda b, j: (jnp.where(j == 0, 0, jnp.where(j == nq - 1, 2, jnp.where(j == nq, 3, 1))),
                            0, 0, 0)
    in_specs = ([pl.BlockSpec((TQ, w), qmap),
                 pl.BlockSpec((TK, w), lambda b, j: (n_lat_blocks + b, 0))]
                + [pl.BlockSpec((TK, w), kblock(i)) for i in range(NA_K_TILES)]
                + [pl.BlockSpec((1, nv, TK), lambda b, j: (n_lat_blocks + b, 0, 0))]
                + [pl.BlockSpec((1, nv, TK), vblock(i)) for i in range(NA_K_TILES)]
                + [pl.BlockSpec((1,) + bias.shape[1:], variant)])
    return pl.pallas_call(
        _na_attn_kernel,
        grid=(n_batch, nq + 1),
        in_specs=in_specs,
        out_specs=pl.BlockSpec((TQ, BRANCH_W), qmap),
        out_shape=jax.ShapeDtypeStruct((t, BRANCH_W), BF16),
        scratch_shapes=[pltpu.VMEM((BRANCH_W, TQ), F32)],
        compiler_params=_cparams(2),
        name="na_attn",
    )(q, k, k, k, k, vt, vt, vt, vt, bias)


GQA_KBLK = WINDOW


def _gqa_attn_kernel(sink_ref, q_ref, kc_ref, k0_ref, k1_ref, k2_ref, k3_ref,
                     vc_ref, v0_ref, v1_ref, v2_ref, v3_ref, o_ref, ot_scr, *, nq):
    j = pl.program_id(1)
    k_refs = (kc_ref, k0_ref, k1_ref, k2_ref, k3_ref)
    v_refs = (vc_ref, v0_ref, v1_ref, v2_ref, v3_ref)
    key_i = lax.broadcasted_iota(jnp.int32, (GQA_KBLK, TQ), 0)
    qry_i = lax.broadcasted_iota(jnp.int32, (GQA_KBLK, TQ), 1)
    keeps = []
    for i in range(4):
        blk = 2 * j - 1 + i
        blk_ok = (blk >= 0) & (blk < 2 * nq) & (j < nq)
        reach = jnp.where(blk_ok, WINDOW, -1)
        keeps.append(jnp.abs((i - 1) * GQA_KBLK + key_i - qry_i) <= reach)
    heads_per_kv = N_HEADS // GQA_KV_HEADS
    keys = [[r[:, _lanes(kv)] for r in k_refs] for kv in range(GQA_KV_HEADS)]
    scores = {}

    def produce(hd):
        kv = hd // heads_per_kv
        qm = _lane_masked(q_ref[:, _lanes(kv)], GQA_HEAD * (hd % heads_per_kv), GQA_HEAD)
        sc = [_nt_dot(kt, qm) for kt in keys[kv]]
        scores[hd] = [sc[0]] + [jnp.where(keeps[i], sc[1 + i], NEG_INF) for i in range(4)]

    def consume(hd):
        sink = jnp.full((1, TQ), sink_ref[hd] * LOG2E, F32)
        _, sink_weight, probs = _softmax_tiles(scores[hd], sink)
        acc = _pv([r[0, _vrows(hd // heads_per_kv), :] for r in v_refs], probs)
        ot_scr[hd * HEAD_V:(hd + 1) * HEAD_V, :] = _normalised(acc, sink_weight)

    _interleave(N_HEADS, produce, consume, N_HEADS)
    o_ref[...] = ot_scr[...].T.astype(BF16)


def _gqa_attn(q, k, vt, sink, *, seq, n_batch):
    t, w = q.shape
    nq = seq // TQ
    n_lat_blocks = n_batch * nq
    nv = vt.shape[1]
    qmap = _q_block_map(nq, n_batch)

    per_tile = TK // GQA_KBLK

    def blk(j, i):
        return jnp.clip(per_tile * j - 1 + i, 0, per_tile * nq - 1)

    def kblock(i):
        return lambda b, j: (b * nq * per_tile + blk(j, i), 0)

    def vblock(i):
        return lambda b, j: (b * nq + blk(j, i) // per_tile, 0, blk(j, i) % per_tile)

    in_specs = ([pl.BlockSpec(memory_space=pltpu.SMEM),
                 pl.BlockSpec((TQ, w), qmap),
                 pl.BlockSpec((TK, w), lambda b, j: (n_lat_blocks + b, 0))]
                + [pl.BlockSpec((GQA_KBLK, w), kblock(i)) for i in range(4)]
                + [pl.BlockSpec((1, nv, TK), lambda b, j: (n_lat_blocks + b, 0, 0))]
                + [pl.BlockSpec((1, nv, GQA_KBLK), vblock(i)) for i in range(4)])
    return pl.pallas_call(
        functools.partial(_gqa_attn_kernel, nq=nq),
        grid=(n_batch, nq + 1),
        in_specs=in_specs,
        out_specs=pl.BlockSpec((TQ, BRANCH_W), qmap),
        out_shape=jax.ShapeDtypeStruct((t, BRANCH_W), BF16),
        scratch_shapes=[pltpu.VMEM((BRANCH_W, TQ), F32)],
        compiler_params=_cparams(2),
        name="gqa_attn",
    )(sink, q, k, k, k, k, k, vt, vt, vt, vt, vt)


def _merge_kernel(h_ref, mod_ref, g_ref, ya_ref, yb_ref, yc_ref, yd_ref, wg_ref, bg_ref, wb_ref, wo_ref,
                  o_ref):
    h = h_ref[...]
    mod = mod_ref[0]
    u = _modulate(h, g_ref[...], mod[3:4], mod[4:5]).astype(BF16)
    merged = None
    for i, y_ref in enumerate((ya_ref, yb_ref, yc_ref, yd_ref)):
        gate = jax.nn.sigmoid(jnp.dot(u, wg_ref[i], preferred_element_type=F32) + bg_ref[i])
        term = gate * jnp.dot(y_ref[...], wb_ref[i], preferred_element_type=F32)
        merged = term if merged is None else merged + term
    y = jnp.dot(merged.astype(BF16), wo_ref[...], preferred_element_type=F32)
    o_ref[...] = h + mod[5:6] * y


def _merge(h, mod_l, g, ys, wg, bg, wb, wo, *, seq, n_batch):
    t, d = h.shape
    tok = lambda i: (i, 0)
    const2 = lambda i: (0, 0)
    const3 = lambda i: (0, 0, 0)
    in_specs = ([pl.BlockSpec((TM_MERGE, d), tok),
                 pl.BlockSpec((1, N_MOD, d), lambda i: (jnp.minimum(i * TM_MERGE // seq, n_batch), 0, 0)),
                 pl.BlockSpec((1, d), const2)]
                + [pl.BlockSpec((TM_MERGE, BRANCH_W), tok)] * 4
                + [pl.BlockSpec(wg.shape, const3), pl.BlockSpec(bg.shape, const3),
                   pl.BlockSpec(wb.shape, const3), pl.BlockSpec(wo.shape, const2)])
    return pl.pallas_call(
        _merge_kernel,
        grid=(t // TM_MERGE,),
        in_specs=in_specs,
        out_specs=pl.BlockSpec((TM_MERGE, d), tok),
        out_shape=jax.ShapeDtypeStruct((t, d), F32),
        compiler_params=_cparams(1),
        name="merge",
    )(h, mod_l, g, *ys, wg, bg, wb, wo)


def _rope_tables(seq, dim):
    tpos = jnp.arange(seq)
    rows = (tpos // GRID_W).astype(F32)
    cols = (tpos % GRID_W).astype(F32)
    half = dim // 2
    freqs = jnp.power(ROPE_BASE, -jnp.arange(0, half, 2, dtype=F32) / half)
    ang = jnp.concatenate([rows[:, None] * freqs] * 2 + [cols[:, None] * freqs] * 2, axis=1)
    nfreq = dim // 4
    sign = np.where((np.arange(dim) % (2 * nfreq)) < nfreq, -1.0, 1.0).astype(np.float32)
    reps = LANES // dim
    cos = jnp.tile(jnp.cos(ang), (1, reps))
    sin = jnp.tile(jnp.sin(ang) * sign, (1, reps))
    cos = jnp.concatenate([cos, jnp.ones((TM, LANES), F32)], axis=0)
    sin = jnp.concatenate([sin, jnp.zeros((TM, LANES), F32)], axis=0)
    return cos, sin


def _vt_weight(w_cols, n_heads):
    d_in = w_cols.shape[0]
    wt = w_cols.T.reshape(n_heads, HEAD_V, d_in)
    wt = jnp.pad(wt, ((0, 0), (0, VROWS - HEAD_V), (0, 0))).reshape(n_heads * VROWS, d_in)
    bias = np.zeros((n_heads, VROWS, 1), np.float32)
    bias[:, ONES_ROW] = 1.0
    return wt, jnp.asarray(bias.reshape(n_heads * VROWS, 1))


def _mixin_weights(w_in, q_norm, w_uq, kv_norm, w_ukv):
    offs = np.concatenate([[0], np.cumsum(IN_SIZES)])
    col = lambda i: w_in[:, offs[i]:offs[i + 1]]
    d = w_in.shape[0]
    kr = jnp.zeros((d, LANES), F32).at[:, MLA_NOPE:MLA_NOPE + MLA_ROPE].set(col(2))
    gk = col(10).reshape(d, GQA_KV_HEADS, 1, GQA_HEAD)
    gk2 = jnp.broadcast_to(gk, (d, GQA_KV_HEADS, 2, GQA_HEAD)).reshape(d, 2 * GQA_KV_HEADS * GQA_HEAD)
    wmain = jnp.concatenate([col(0), col(1), kr, col(3) * (DIFF_SCALE * LOG2E), col(4),
                             col(6) * (NA_SCALE * LOG2E), col(7), col(9) * (GQA_SCALE * LOG2E), gk2], axis=1)
    wt_b, bt_b = _vt_weight(col(5), N_HEADS)
    wt_c, bt_c = _vt_weight(col(8), N_HEADS)
    wt_d, bt_d = _vt_weight(col(11), GQA_KV_HEADS)
    qk = MLA_NOPE + MLA_ROPE
    wq2 = jnp.pad((w_uq * (MLA_SCALE * LOG2E)).reshape(MLA_Q_RANK, N_HEADS, qk),
                  ((0, 0), (0, 0), (0, LANES - qk))).reshape(MLA_Q_RANK, N_HEADS * LANES)
    ukv = w_ukv.reshape(MLA_KV_RANK, N_HEADS, MLA_NOPE + HEAD_V)
    wk2 = jnp.pad(ukv[:, :, :MLA_NOPE], ((0, 0), (0, 0), (0, LANES - MLA_NOPE))
                  ).reshape(MLA_KV_RANK, N_HEADS * LANES)
    wv2, bv2 = _vt_weight(ukv[:, :, MLA_NOPE:].reshape(MLA_KV_RANK, N_HEADS * HEAD_V), N_HEADS)
    return {"wmain": wmain.astype(BF16),
            "wt": jnp.concatenate([wt_b, wt_c, wt_d], axis=0).astype(BF16),
            "bt": jnp.concatenate([bt_b, bt_c, bt_d], axis=0),
            "qn": q_norm.reshape(1, -1), "kvn": kv_norm.reshape(1, -1),
            "wq2": wq2.astype(BF16), "wk2": wk2.astype(BF16), "wv2": wv2.astype(BF16), "bv2": bv2}


def _ffn_weights(w_gu, w_down):
    d, two_f = w_gu.shape
    d_ff = two_f // 2
    n_chunks = d_ff // FF_CHUNK
    gate = w_gu[:, :d_ff].reshape(d, n_chunks, FF_CHUNK)
    up = w_gu[:, d_ff:].reshape(d, n_chunks, FF_CHUNK)
    wgu_c = jnp.transpose(jnp.concatenate([gate, up], axis=2), (1, 0, 2)).astype(BF16)
    return wgu_c, w_down.astype(BF16)


def kernel(x, c, ctx, c_ctx, w_ada, b_ada, norm_ffn1, ffn1_w_gu, ffn1_w_down, norm_mix, w_in, mla_q_norm,
           mla_w_uq, mla_kv_norm, mla_w_ukv, diff_lam, diff_subln, na_rpb, gqa_sink, w_branch, w_gate,
           b_gate, w_out, norm_ffn2, ffn2_w_gu, ffn2_w_down, final_norm):
    n_batch, seq, d = x.shape
    ctx_len = ctx.shape[1]
    depth = w_ada.shape[0]
    rows = seq // GRID_W
    assert ctx_len == TK and all(seq % tm == 0 and (n_batch * ctx_len) % tm == 0 for tm in (TM, TM_MERGE))
    assert rows % NA_Q_ROWS == 0 and rows >= NA_K_TILES * NA_Q_ROWS and seq // TQ >= 3
    n_lat = n_batch * seq
    dims = dict(seq=seq, n_batch=n_batch)

    r_pad = -(-(n_batch + 1) // 8) * 8
    cc = jnp.zeros((r_pad, d), F32).at[:n_batch].set(c).at[n_batch].set(c_ctx)
    mod = _modulation(cc, w_ada, b_ada).reshape(depth, r_pad, N_MOD, d)

    tabs = _rope_tables(seq, MLA_ROPE) + _rope_tables(seq, GQA_HEAD)
    h = (x.reshape(n_lat, d), ctx.reshape(n_batch * ctx_len, d))
    n_tok = n_lat + n_batch * ctx_len

    for l in range(depth):
        last = l == depth - 1
        lam_init = 0.8 - 0.6 * math.exp(-0.3 * l)
        mod_l = mod[l]
        row = lambda v: v.reshape(1, -1)
        wgu1, wdn1 = _ffn_weights(ffn1_w_gu[l], ffn1_w_down[l])
        wgu2, wdn2 = _ffn_weights(ffn2_w_gu[l], ffn2_w_down[l])

        h = _ffn(h, mod_l, row(norm_ffn1[l]), wgu1, wdn1, mod_base=0, n_tok_out=n_tok, **dims)

        mw = _mixin_weights(w_in[l], mla_q_norm[l], mla_w_uq[l], mla_kv_norm[l], mla_w_ukv[l])
        (qa, ka, vat, qb, kb, vbt, qc, kc, vct, qd, kd, vdt) = _mixin(
            h, mod_l, row(norm_mix[l]), tabs, mw, **dims)
        ya = _dense_attn(qa, ka, vat, diff=False, **dims)
        yb = _dense_attn(qb, kb, vbt, diff=True, lam_init=lam_init, lamp=diff_lam[l].astype(F32),
                         subln=diff_subln[l].reshape(HEAD_V, 1), **dims)
        yc = _na_attn(qc, kc, vct, _na_bias_tables(na_rpb[l], rows), **dims)
        yd = _gqa_attn(qd, kd, vdt, gqa_sink[l].astype(F32), **dims)
        h = _merge(h, mod_l, row(norm_mix[l]), (ya, yb, yc, yd), w_gate[l].astype(BF16),
                   b_gate[l].reshape(4, 1, d), w_branch[l].astype(BF16), w_out[l].astype(BF16), **dims)

        h = _ffn(h, mod_l, row(norm_ffn2[l]), wgu2, wdn2, mod_base=6,
                 n_tok_out=n_lat if last else n_tok, final_g=row(final_norm) if last else None, **dims)
    return h.reshape(n_batch, seq, d)
```
